```python
import math
import jax, jax.numpy as jnp
from jax import lax
import numpy as np

D_MODEL = 2048
BATCH = 1
SEQ = 16384
DEPTH = 4
DEC_BATCH = 4
DEC_SEQ = 8192
PAST_LEN = 128

GRID_W = 64
BLOCK = 128
HEAD_DIM = 128
A_HEADS = 8
A_KV_HEADS = 2
WINDOW = 128
B_HEADS = 8
B_KV_HEADS = 2
C_HEADS = 8
C_Q_RANK = 512
C_KV_RANK = 512
C_NOPE = 128
C_ROPE = 64
C_V = 128
ROPE_THETA = 10000.0
N_BUCKETS = 32
MAX_DIST = 128
N_EXPERTS = 16
EXPERT_FF = 2048
EC_FACTOR = 2
N_BRANCH = 3
DN_ALPHA = (2 * DEPTH) ** 0.25
DN_BETA = (8 * DEPTH) ** -0.25
EPS = 1e-6
NEG = -1e30

IN_WIDTHS = (
    A_HEADS * HEAD_DIM, A_KV_HEADS * HEAD_DIM, A_KV_HEADS * HEAD_DIM,
    B_HEADS * HEAD_DIM, B_KV_HEADS * HEAD_DIM, B_KV_HEADS * HEAD_DIM,
    C_Q_RANK, C_KV_RANK, C_ROPE,
    N_BRANCH * D_MODEL,
)
IN_DIM = sum(IN_WIDTHS)
IN_SPLITS = tuple(int(s) for s in np.cumsum(IN_WIDTHS)[:-1])

kernel_name = 'hybrid_bidir_encoder_gated_branches'


def layer_norm(x, g, b):
    xf = x.astype(jnp.float32)
    xc = xf - xf.mean(-1, keepdims=True)
    var = jnp.mean(xc * xc, -1, keepdims=True)
    return (xc * lax.rsqrt(var + EPS) * g.astype(jnp.float32) + b.astype(jnp.float32)).astype(x.dtype)


def rms_norm(x, g):
    xf = x.astype(jnp.float32)
    return (xf * lax.rsqrt(jnp.mean(xf * xf, -1, keepdims=True) + EPS) * g.astype(jnp.float32)).astype(x.dtype)


def split_heads(z, n):
    return z.reshape(z.shape[0], z.shape[1], n, -1)


def rope_table(pos, dim):
    inv = ROPE_THETA ** (-jnp.arange(0, dim, 2, dtype=jnp.float32) / dim)
    ang = pos.astype(jnp.float32)[:, None] * inv[None, :]
    return jnp.cos(ang), jnp.sin(ang)


def apply_rope(x, cos, sin):
    half = x.shape[-1] // 2
    xf = x.astype(jnp.float32)
    x1, x2 = xf[..., :half], xf[..., half:]
    c, s = cos[None, :, None, :], sin[None, :, None, :]
    return jnp.concatenate([x1 * c - x2 * s, x1 * s + x2 * c], -1).astype(x.dtype)


def axial_rope(z, cos_r, sin_r, cos_c, sin_c):
    hb = z.shape[-1] // 2
    return jnp.concatenate([apply_rope(z[..., :hb], cos_r, sin_r), apply_rope(z[..., hb:], cos_c, sin_c)], -1)


def t5_bucket(rel):
    nb = N_BUCKETS // 2
    max_exact = nb // 2
    n = jnp.abs(rel)
    scaled = jnp.log(jnp.maximum(n, 1).astype(jnp.float32) / max_exact) / math.log(MAX_DIST / max_exact)
    large = jnp.minimum(max_exact + (scaled * (nb - max_exact)).astype(jnp.int32), nb - 1)
    return jnp.where(rel > 0, nb, 0) + jnp.where(n < max_exact, n, large)


def window_attention(q, k, v, sink, rel_bias, offs):
    bsz, seq, hq, d = q.shape
    hk = k.shape[2]
    g = hq // hk
    nb = seq // BLOCK
    qb = q.reshape(bsz, nb, BLOCK, hk, g, d)

    def band(z):
        zp = jnp.pad(z, ((0, 0), (BLOCK, BLOCK), (0, 0), (0, 0))).reshape(bsz, nb + 2, BLOCK, hk, z.shape[-1])
        return jnp.concatenate([zp[:, :-2], zp[:, 1:-1], zp[:, 2:]], axis=2)

    kb, vb = band(k), band(v)
    s = jnp.einsum('bnqhgd,bnkhd->bnhgqk', qb, kb, preferred_element_type=jnp.float32) * (d ** -0.5)
    s = s + rel_bias.astype(jnp.float32).reshape(hk, g, BLOCK, 3 * BLOCK)
    key_pos = jnp.arange(nb)[:, None] * BLOCK - BLOCK + jnp.arange(3 * BLOCK)[None, :]
    valid = (jnp.abs(offs) <= WINDOW)[None] & ((key_pos >= 0) & (key_pos < seq))[:, None, :]
    s = jnp.where(valid[None, :, None, None], s, NEG)
    sink_l = sink.astype(jnp.float32).reshape(hk, g)[None, None, :, :, None]
    m = jnp.maximum(s.max(-1), sink_l)
    p = jnp.exp(s - m[..., None])
    p = p / (p.sum(-1) + jnp.exp(sink_l - m))[..., None]
    o = jnp.einsum('bnhgqk,bnkhd->bnqhgd', p.astype(v.dtype), vb)
    return o.reshape(bsz, seq, hq, d)


def dense_attention(q, k, v):
    bsz, seq, hq, dk = q.shape
    hk = k.shape[2]
    g = hq // hk
    nb = seq // BLOCK
    scale = dk ** -0.5
    qb = jnp.swapaxes(q.reshape(bsz, nb, BLOCK, hk, g, dk), 0, 1)

    def one_block(qblk):
        s = jnp.einsum('bqhgd,bkhd->bhgqk', qblk, k, preferred_element_type=jnp.float32) * scale
        p = jax.nn.softmax(s, axis=-1)
        return jnp.einsum('bhgqk,bkhe->bqhge', p.astype(v.dtype), v)

    o = lax.map(one_block, qb)
    return jnp.swapaxes(o, 0, 1).reshape(bsz, seq, hq, v.shape[-1])


def expert_choice_ffn(x, w_router, w_gate, w_up, w_down):
    bsz, seq, d = x.shape
    n_tok = bsz * seq
    cap = EC_FACTOR * n_tok // N_EXPERTS
    xt = x.reshape(n_tok, d)
    aff = jax.nn.softmax(jnp.matmul(xt, w_router, preferred_element_type=jnp.float32), axis=-1)
    gval, idx = lax.top_k(aff.T, cap)
    xe = xt[idx]
    h = jax.nn.silu(jnp.einsum('ecd,edf->ecf', xe, w_gate)) * jnp.einsum('ecd,edf->ecf', xe, w_up)
    ye = jnp.einsum('ecf,efd->ecd', h, w_down) * gval[..., None].astype(x.dtype)
    out = jnp.zeros_like(xt).at[idx.reshape(-1)].add(ye.reshape(-1, d))
    return out.reshape(bsz, seq, d)


def encoder_trunk(x, rel_table, w_in, b_gate, b_q_norm, b_k_norm, c_q_norm, c_kv_norm, w_uq, w_ukv, a_sink,
                  w_branch_a, w_branch_b, w_branch_c, w_o, ln1_g, ln1_b, w_router, w_e_gate, w_e_up, w_e_down,
                  ln2_g, ln2_b):
    bsz, seq, _ = x.shape
    rows = seq // GRID_W
    t = jnp.arange(seq)
    row = jnp.repeat(jnp.arange(rows), GRID_W)
    col = jnp.tile(jnp.arange(GRID_W), rows)
    cos_r, sin_r = rope_table(row, HEAD_DIM // 2)
    cos_c, sin_c = rope_table(col, HEAD_DIM // 2)
    cos_t, sin_t = rope_table(t, C_ROPE)
    offs = jnp.arange(3 * BLOCK)[None, :] - BLOCK - jnp.arange(BLOCK)[:, None]
    rel_bias = jnp.transpose(rel_table[t5_bucket(offs)], (2, 0, 1))
    for l in range(DEPTH):
        h = jnp.matmul(x, w_in[l])
        qa, ka, va, qb, kb, vb, cq, ckv, kr, gpre = jnp.split(h, IN_SPLITS, axis=-1)
        oa = window_attention(split_heads(qa, A_HEADS), split_heads(ka, A_KV_HEADS), split_heads(va, A_KV_HEADS),
                              a_sink[l], rel_bias, offs)
        qb = axial_rope(rms_norm(split_heads(qb, B_HEADS), b_q_norm[l]), cos_r, sin_r, cos_c, sin_c)
        kb = axial_rope(rms_norm(split_heads(kb, B_KV_HEADS), b_k_norm[l]), cos_r, sin_r, cos_c, sin_c)
        ob = dense_attention(qb, kb, split_heads(vb, B_KV_HEADS))
        qc = split_heads(jnp.matmul(rms_norm(cq, c_q_norm[l]), w_uq[l]), C_HEADS)
        kvc = split_heads(jnp.matmul(rms_norm(ckv, c_kv_norm[l]), w_ukv[l]), C_HEADS)
        k_rope = apply_rope(kr[:, :, None, :], cos_t, sin_t)
        qc = jnp.concatenate([qc[..., :C_NOPE], apply_rope(qc[..., C_NOPE:], cos_t, sin_t)], -1)
        kc = jnp.concatenate([kvc[..., :C_NOPE], jnp.broadcast_to(k_rope, (bsz, seq, C_HEADS, C_ROPE))], -1)
        oc = dense_attention(qc, kc, kvc[..., C_NOPE:])
        ga, gb, gc = jnp.split(jax.nn.sigmoid(gpre + b_gate[l]), N_BRANCH, axis=-1)
        merged = (ga * jnp.matmul(oa.reshape(bsz, seq, -1), w_branch_a[l])
                  + gb * jnp.matmul(ob.reshape(bsz, seq, -1), w_branch_b[l])
                  + gc * jnp.matmul(oc.reshape(bsz, seq, -1), w_branch_c[l]))
        x = layer_norm(DN_ALPHA * x + jnp.matmul(merged, w_o[l]), ln1_g[l], ln1_b[l])
        x = layer_norm(DN_ALPHA * x + expert_choice_ffn(x, w_router[l], w_e_gate[l], w_e_up[l], w_e_down[l]),
                       ln2_g[l], ln2_b[l])
    return x


def setup_inputs(seed: int = 0) -> dict:
    key = jax.random.key(seed)
    ks = jax.random.split(key, 26)
    f32 = jnp.float32
    L, D, E, F = DEPTH, D_MODEL, N_EXPERTS, EXPERT_FF
    nrm = lambda k, shape, scale: jax.random.normal(k, shape, f32) * scale
    gain = lambda k, shape: 1.0 + 0.02 * jax.random.normal(k, shape, f32)
    return {
        'x_prompt': nrm(ks[0], (BATCH, SEQ, D), 1.0),
        'x_sample': nrm(ks[1], (DEC_BATCH, DEC_SEQ, D), 1.0),
        'rel_table': nrm(ks[2], (N_BUCKETS, A_HEADS), 0.5),
        'w_in': nrm(ks[3], (L, D, IN_DIM), D ** -0.5),
        'b_gate': nrm(ks[4], (L, N_BRANCH * D), 0.1),
        'b_q_norm': gain(ks[5], (L, HEAD_DIM)),
        'b_k_norm': gain(ks[6], (L, HEAD_DIM)),
        'c_q_norm': gain(ks[7], (L, C_Q_RANK)),
        'c_kv_norm': gain(ks[8], (L, C_KV_RANK)),
        'w_uq': nrm(ks[9], (L, C_Q_RANK, C_HEADS * (C_NOPE + C_ROPE)), C_Q_RANK ** -0.5),
        'w_ukv': nrm(ks[10], (L, C_KV_RANK, C_HEADS * (C_NOPE + C_V)), C_KV_RANK ** -0.5),
        'a_sink': nrm(ks[11], (L, A_HEADS), 0.5),
        'w_branch_a': nrm(ks[12], (L, A_HEADS * HEAD_DIM, D), (A_HEADS * HEAD_DIM) ** -0.5),
        'w_branch_b': nrm(ks[13], (L, B_HEADS * HEAD_DIM, D), (B_HEADS * HEAD_DIM) ** -0.5),
        'w_branch_c': nrm(ks[14], (L, C_HEADS * C_V, D), (C_HEADS * C_V) ** -0.5),
        'w_o': nrm(ks[15], (L, D, D), D ** -0.5 * DN_BETA),
        'ln1_g': gain(ks[16], (L, D)),
        'ln1_b': nrm(ks[17], (L, D), 0.02),
        'w_router': nrm(ks[18], (L, D, E), D ** -0.5),
        'w_e_gate': nrm(ks[19], (L, E, D, F), D ** -0.5),
        'w_e_up': nrm(ks[20], (L, E, D, F), D ** -0.5),
        'w_e_down': nrm(ks[21], (L, E, F, D), F ** -0.5 * DN_BETA),
        'ln2_g': gain(ks[22], (L, D)),
        'ln2_b': nrm(ks[23], (L, D), 0.02),
    }


def reference(x_prompt, x_sample, rel_table, w_in, b_gate, b_q_norm, b_k_norm, c_q_norm, c_kv_norm, w_uq, w_ukv,
              a_sink, w_branch_a, w_branch_b, w_branch_c, w_o, ln1_g, ln1_b, w_router, w_e_gate, w_e_up, w_e_down,
              ln2_g, ln2_b):
    y_prompt = encoder_trunk(x_prompt, rel_table, w_in, b_gate, b_q_norm, b_k_norm, c_q_norm, c_kv_norm, w_uq, w_ukv,
                             a_sink, w_branch_a, w_branch_b, w_branch_c, w_o, ln1_g, ln1_b, w_router, w_e_gate,
                             w_e_up, w_e_down, ln2_g, ln2_b)
    y_sample = encoder_trunk(x_sample, rel_table, w_in, b_gate, b_q_norm, b_k_norm, c_q_norm, c_kv_norm, w_uq, w_ukv,
                             a_sink, w_branch_a, w_branch_b, w_branch_c, w_o, ln1_g, ln1_b, w_router, w_e_gate,
                             w_e_up, w_e_down, ln2_g, ln2_b)
    return (y_prompt, y_sample)
```

```python
import functools
import math

import jax
import jax.numpy as jnp
import numpy as np
from jax import lax
from jax.experimental import pallas as pl
from jax.experimental.pallas import tpu as pltpu

D_MODEL = 2048
DEPTH = 4
GRID_W = 64
BLOCK = 128
HEAD_DIM = 128
A_HEADS = 8
A_KV_HEADS = 2
WINDOW = 128
B_HEADS = 8
B_KV_HEADS = 2
C_HEADS = 8
C_Q_RANK = 512
C_KV_RANK = 512
C_NOPE = 128
C_ROPE = 64
C_V = 128
ROPE_THETA = 10000.0
N_BUCKETS = 32
MAX_DIST = 128
N_EXPERTS = 16
EXPERT_FF = 2048
EC_FACTOR = 2
N_BRANCH = 3
EPS = 1e-6
NEG = -1e30
LOG2E = 1.4426950408889634

LANE = 128
C_PAD = 2 * LANE
VMEM_LIMIT = 56 * 1024 * 1024

BF16 = jnp.bfloat16
F32 = jnp.float32


def _dn_alpha():
    return (2 * DEPTH) ** 0.25


def _layout():
    d = D_MODEL
    segs = [('g', N_BRANCH * d), ('qa', A_HEADS * HEAD_DIM), ('qb', B_HEADS * HEAD_DIM), ('cq', C_Q_RANK),
            ('ckv', C_KV_RANK), ('ka', A_KV_HEADS * HEAD_DIM), ('va', A_KV_HEADS * HEAD_DIM),
            ('kb', B_KV_HEADS * HEAD_DIM), ('vb', B_KV_HEADS * HEAD_DIM), ('kr', C_PAD)]
    off, o = {}, 0
    for name, w in segs:
        off[name] = (o, w)
        o += w
    total = -(-o // 512) * 512
    return off, total


def _cparams(sem, vmem=VMEM_LIMIT):
    return pltpu.CompilerParams(dimension_semantics=sem, vmem_limit_bytes=vmem)


def _mm_kernel(x_ref, w_ref, o_ref):
    o_ref[...] = jnp.dot(x_ref[...], w_ref[...], preferred_element_type=F32).astype(o_ref.dtype)


def _matmul(x, w, out_dtype, tm, tn):
    m, k = x.shape
    n = w.shape[1]
    tm, tn = min(tm, m), min(tn, n)
    return pl.pallas_call(
        _mm_kernel,
        grid=(m // tm, n // tn),
        in_specs=[pl.BlockSpec((tm, k), lambda i, j: (i, 0)), pl.BlockSpec((k, tn), lambda i, j: (0, j))],
        out_specs=pl.BlockSpec((tm, tn), lambda i, j: (i, j)),
        out_shape=jax.ShapeDtypeStruct((m, n), out_dtype),
        compiler_params=_cparams(("parallel", "parallel")),
        name="in_proj",
    )(x, w)


def _rope(x, cos, sa, sb):
    return x * cos + pltpu.roll(x, LANE - 32, 1) * sa + pltpu.roll(x, 32, 1) * sb


def _rms(x, g):
    return x * lax.rsqrt(jnp.mean(x * x, axis=-1, keepdims=True) + EPS) * g


def _bprep_kernel(q_ref, k_ref, cos_ref, sa_ref, sb_ref, gq_ref, gk_ref, qo_ref, ko_ref, *, scale):
    cos, sa, sb = cos_ref[...], sa_ref[...], sb_ref[...]
    gq, gk = gq_ref[...], gk_ref[...]
    for h in range(B_HEADS):
        x = q_ref[:, h * LANE:(h + 1) * LANE].astype(F32)
        qo_ref[:, h * LANE:(h + 1) * LANE] = (_rope(_rms(x, gq), cos, sa, sb) * scale).astype(BF16)
    for h in range(B_KV_HEADS):
        x = k_ref[:, h * LANE:(h + 1) * LANE].astype(F32)
        ko_ref[:, h * LANE:(h + 1) * LANE] = _rope(_rms(x, gk), cos, sa, sb).astype(BF16)


def _b_prep(h, tabs, gq, gk, seq, tm):
    off, _ = _layout()
    n = h.shape[0]
    tm = min(tm, seq)
    spb = seq // tm
    qw, kw = B_HEADS * HEAD_DIM, B_KV_HEADS * HEAD_DIM
    qo, ko = off['qb'][0] // qw, off['kb'][0] // kw
    tab_spec = pl.BlockSpec((tm, LANE), lambda i: (i % spb, 0))
    vec_spec = pl.BlockSpec((1, LANE), lambda i: (0, 0))
    scale = HEAD_DIM ** -0.5 * LOG2E
    return pl.pallas_call(
        functools.partial(_bprep_kernel, scale=scale),
        grid=(n // tm,),
        in_specs=[pl.BlockSpec((tm, qw), lambda i: (i, qo)), pl.BlockSpec((tm, kw), lambda i: (i, ko)),
                  tab_spec, tab_spec, tab_spec, vec_spec, vec_spec],
        out_specs=[pl.BlockSpec((tm, qw), lambda i: (i, 0)), pl.BlockSpec((tm, kw), lambda i: (i, 0))],
        out_shape=[jax.ShapeDtypeStruct((n, qw), BF16), jax.ShapeDtypeStruct((n, kw), BF16)],
        compiler_params=_cparams(("parallel",)),
        name="b_prep",
    )(h, h, *tabs, gq, gk)


def _cprep_kernel(cq_ref, ckv_ref, kr_ref, wq_ref, wk_ref, wv_ref, gq_ref, gkv_ref, cos_ref, sa_ref, sb_ref,
                  qo_ref, ko_ref, vo_ref, *, scale):
    cos, sa, sb = cos_ref[...], sa_ref[...], sb_ref[...]
    cqn = _rms(cq_ref[...].astype(F32), gq_ref[...]).astype(BF16)
    ckvn = _rms(ckv_ref[...].astype(F32), gkv_ref[...]).astype(BF16)
    q = jnp.dot(cqn, wq_ref[...], preferred_element_type=F32)
    kn = jnp.dot(ckvn, wk_ref[...], preferred_element_type=F32)
    vo_ref[...] = jnp.dot(ckvn, wv_ref[...], preferred_element_type=F32).astype(BF16)
    kr = _rope(kr_ref[:, LANE:].astype(F32), cos, sa, sb).astype(BF16)
    for h in range(C_HEADS):
        b = h * C_PAD
        qo_ref[:, b:b + LANE] = (q[:, b:b + LANE] * scale).astype(BF16)
        qo_ref[:, b + LANE:b + C_PAD] = (_rope(q[:, b + LANE:b + C_PAD], cos, sa, sb) * scale).astype(BF16)
        ko_ref[:, b:b + LANE] = kn[:, h * LANE:(h + 1) * LANE].astype(BF16)
        ko_ref[:, b + LANE:b + C_PAD] = kr


def _c_prep(h, wq, wk, wv, gq, gkv, tabs, seq, tm):
    off, _ = _layout()
    n = h.shape[0]
    tm = min(tm, seq)
    spb = seq // tm
    cqo, ckvo, kro = off['cq'][0] // C_Q_RANK, off['ckv'][0] // C_KV_RANK, off['kr'][0] // C_PAD
    tab_spec = pl.BlockSpec((tm, LANE), lambda i: (i % spb, 0))
    full = lambda a: pl.BlockSpec(a.shape, lambda i: (0, 0))
    scale = (C_NOPE + C_ROPE) ** -0.5 * LOG2E
    qw, vw = C_HEADS * C_PAD, C_HEADS * C_V
    return pl.pallas_call(
        functools.partial(_cprep_kernel, scale=scale),
        grid=(n // tm,),
        in_specs=[pl.BlockSpec((tm, C_Q_RANK), lambda i: (i, cqo)), pl.BlockSpec((tm, C_KV_RANK), lambda i: (i, ckvo)),
                  pl.BlockSpec((tm, C_PAD), lambda i: (i, kro)), full(wq), full(wk), full(wv), full(gq), full(gkv),
                  tab_spec, tab_spec, tab_spec],
        out_specs=[pl.BlockSpec((tm, qw), lambda i: (i, 0)), pl.BlockSpec((tm, qw), lambda i: (i, 0)),
                   pl.BlockSpec((tm, vw), lambda i: (i, 0))],
        out_shape=[jax.ShapeDtypeStruct((n, qw), BF16), jax.ShapeDtypeStruct((n, qw), BF16),
                   jax.ShapeDtypeStruct((n, vw), BF16)],
        compiler_params=_cparams(("parallel",)),
        name="c_prep",
    )(h, h, h, wq, wk, wv, gq, gkv, *tabs)


def _win_kernel(q_ref, kp_ref, kc_ref, kn_ref, vp_ref, vc_ref, vn_ref, bias_ref, sink_ref, o_ref, *, seq, scale):
    nblk = pl.program_id(1)
    g = A_HEADS // A_KV_HEADS
    rows = g * BLOCK
    qi = lax.broadcasted_iota(jnp.int32, (rows, 3 * BLOCK), 0) & (BLOCK - 1)
    ci = lax.broadcasted_iota(jnp.int32, (rows, 3 * BLOCK), 1)
    offs = ci - BLOCK - qi
    key_pos = nblk * BLOCK - BLOCK + ci
    valid = (jnp.abs(offs) <= WINDOW) & (key_pos >= 0) & (key_pos < seq)
    for hk in range(A_KV_HEADS):
        ks = slice(hk * HEAD_DIM, (hk + 1) * HEAD_DIM)
        k = jnp.concatenate([kp_ref[:, ks], kc_ref[:, ks], kn_ref[:, ks]], axis=0)
        v = jnp.concatenate([vp_ref[:, ks], vc_ref[:, ks], vn_ref[:, ks]], axis=0)
        q = jnp.concatenate([q_ref[:, (hk * g + j) * HEAD_DIM:(hk * g + j + 1) * HEAD_DIM] for j in range(g)],
                            axis=0)
        s = lax.dot_general(q, k, (((1,), (1,)), ((), ())), preferred_element_type=F32) * scale + bias_ref[hk]
        s = jnp.where(valid, s, NEG)
        sink = sink_ref[hk]
        m = jnp.maximum(jnp.max(s, axis=-1, keepdims=True), sink)
        p = jnp.exp(s - m)
        denom = jnp.sum(p, axis=-1, keepdims=True) + jnp.exp(sink - m)
        o = jnp.dot(p.astype(BF16), v, preferred_element_type=F32) / denom
        for j in range(g):
            o_ref[:, (hk * g + j) * HEAD_DIM:(hk * g + j + 1) * HEAD_DIM] = o[j * BLOCK:(j + 1) * BLOCK].astype(BF16)


def _window_attention(h, bias, sink, bsz, seq):
    off, _ = _layout()
    nb = seq // BLOCK
    qw, kw = A_HEADS * HEAD_DIM, A_KV_HEADS * HEAD_DIM
    qo, ko, vo = off['qa'][0] // qw, off['ka'][0] // kw, off['va'][0] // kw
    cur = lambda b, n: b * nb + n
    prv = lambda b, n: b * nb + jnp.maximum(n - 1, 0)
    nxt = lambda b, n: b * nb + jnp.minimum(n + 1, nb - 1)
    kv = lambda f, c: pl.BlockSpec((BLOCK, kw), lambda b, n: (f(b, n), c))
    full = lambda a: pl.BlockSpec(a.shape, lambda b, n: (0,) * a.ndim)
    return pl.pallas_call(
        functools.partial(_win_kernel, seq=seq, scale=HEAD_DIM ** -0.5),
        grid=(bsz, nb),
        in_specs=[pl.BlockSpec((BLOCK, qw), lambda b, n: (cur(b, n), qo)),
                  kv(prv, ko), kv(cur, ko), kv(nxt, ko), kv(prv, vo), kv(cur, vo), kv(nxt, vo),
                  full(bias), full(sink)],
        out_specs=pl.BlockSpec((BLOCK, qw), lambda b, n: (cur(b, n), 0)),
        out_shape=jax.ShapeDtypeStruct((bsz * seq, qw), BF16),
        compiler_params=_cparams(("parallel", "parallel")),
        name="window_attn",
    )(h, h, h, h, h, h, h, bias, sink)


def _flash_kernel(q_ref, k_ref, v_ref, o_ref, m_sc, l_sc, acc_sc):
    ki = pl.program_id(3)

    @pl.when(ki == 0)
    def _():
        m_sc[...] = jnp.full(m_sc.shape, -jnp.inf, F32)
        l_sc[...] = jnp.zeros(l_sc.shape, F32)
        acc_sc[...] = jnp.zeros(acc_sc.shape, F32)

    s = lax.dot_general(q_ref[...], k_ref[...], (((1,), (1,)), ((), ())), preferred_element_type=F32)
    m_prev = m_sc[...]
    m_new = jnp.maximum(m_prev, jnp.max(s, axis=-1, keepdims=True))
    alpha = jnp.exp2(m_prev - m_new)
    p = jnp.exp2(s - m_new)
    l_sc[...] = alpha * l_sc[...] + jnp.sum(p, axis=-1, keepdims=True)
    acc_sc[...] = alpha * acc_sc[...] + jnp.dot(p.astype(BF16), v_ref[...], preferred_element_type=F32)
    m_sc[...] = m_new

    @pl.when(ki == pl.num_programs(3) - 1)
    def _():
        o_ref[...] = (acc_sc[...] / l_sc[...]).astype(o_ref.dtype)


def _flash(q, k, v, bsz, seq, hq, hk, dq, dv, q_col0, k_col0, v_col0, bq, bk):
    bq, bk = min(bq, seq), min(bk, seq)
    nq, nk = seq // bq, seq // bk
    g = hq // hk
    return pl.pallas_call(
        _flash_kernel,
        grid=(bsz, hq, nq, nk),
        in_specs=[pl.BlockSpec((bq, dq), lambda b, h, i, j: (b * nq + i, q_col0 + h)),
                  pl.BlockSpec((bk, dq), lambda b, h, i, j: (b * nk + j, k_col0 + h // g)),
                  pl.BlockSpec((bk, dv), lambda b, h, i, j: (b * nk + j, v_col0 + h // g))],
        out_specs=pl.BlockSpec((bq, dv), lambda b, h, i, j: (b * nq + i, h)),
        out_shape=jax.ShapeDtypeStruct((bsz * seq, hq * dv), BF16),
        scratch_shapes=[pltpu.VMEM((bq, 1), F32), pltpu.VMEM((bq, 1), F32), pltpu.VMEM((bq, dv), F32)],
        compiler_params=_cparams(("parallel", "parallel", "parallel", "arbitrary")),
        name="dense_attn",
    )(q, k, v)


def _merge_kernel(oa_ref, ob_ref, oc_ref, wa_ref, wb_ref, wc_ref, ga_ref, gb_ref, gc_ref, ba_ref, bb_ref, bc_ref,
                  o_ref):
    def branch(o_r, w_r, g_r, b_r):
        gate = jax.nn.sigmoid(g_r[...].astype(F32) + b_r[...])
        return gate * jnp.dot(o_r[...], w_r[...], preferred_element_type=F32)

    o_ref[...] = (branch(oa_ref, wa_ref, ga_ref, ba_ref) + branch(ob_ref, wb_ref, gb_ref, bb_ref)
                  + branch(oc_ref, wc_ref, gc_ref, bc_ref)).astype(o_ref.dtype)


def _merge(oa, ob, oc, wa, wb, wc, h, b_gate, tm, tn):
    n, kdim = oa.shape
    d = D_MODEL
    tm, tn = min(tm, n), min(tn, d)
    nj = d // tn
    act = pl.BlockSpec((tm, kdim), lambda i, j: (i, 0))
    wsp = pl.BlockSpec((kdim, tn), lambda i, j: (0, j))
    gate = lambda br: pl.BlockSpec((tm, tn), lambda i, j: (i, br * nj + j))
    bias = lambda br: pl.BlockSpec((1, tn), lambda i, j: (0, br * nj + j))
    return pl.pallas_call(
        _merge_kernel,
        grid=(n // tm, nj),
        in_specs=[act, act, act, wsp, wsp, wsp, gate(0), gate(1), gate(2), bias(0), bias(1), bias(2)],
        out_specs=pl.BlockSpec((tm, tn), lambda i, j: (i, j)),
        out_shape=jax.ShapeDtypeStruct((n, d), BF16),
        compiler_params=_cparams(("parallel", "parallel")),
        name="branch_merge",
    )(oa, ob, oc, wa, wb, wc, h, h, h, b_gate, b_gate, b_gate)


def _layer_norm(z, g, b):
    zc = z - jnp.mean(z, axis=-1, keepdims=True)
    var = jnp.mean(zc * zc, axis=-1, keepdims=True)
    return zc * lax.rsqrt(var + EPS) * g + b


def _woln_kernel(m_ref, wo_ref, x_ref, g_ref, b_ref, wr_ref, x1_ref, acc_ref, lg_ref, *, alpha):
    y = jnp.dot(m_ref[...], wo_ref[...], preferred_element_type=F32)
    x1 = _layer_norm(alpha * x_ref[...] + y, g_ref[...], b_ref[...])
    x1_ref[...] = x1
    acc_ref[...] = alpha * x1
    lg_ref[...] = jnp.dot(x1, wr_ref[...], preferred_element_type=F32, precision=lax.Precision.HIGHEST)


def _wo_ln(merged, wo, x, g, b, wr, tm):
    n, d = x.shape
    tm = min(tm, n)
    row = lambda w: pl.BlockSpec((tm, w), lambda i: (i, 0))
    full = lambda a: pl.BlockSpec(a.shape, lambda i: (0, 0))
    return pl.pallas_call(
        functools.partial(_woln_kernel, alpha=_dn_alpha()),
        grid=(n // tm,),
        in_specs=[row(d), full(wo), row(d), full(g), full(b), full(wr)],
        out_specs=[row(d), row(d), row(LANE)],
        out_shape=[jax.ShapeDtypeStruct((n, d), F32), jax.ShapeDtypeStruct((n, d), F32),
                   jax.ShapeDtypeStruct((n, LANE), F32)],
        compiler_params=_cparams(("parallel",)),
        name="wo_ln_router",
    )(merged, wo, x, g, b, wr)


def _route_kernel(lg_ref, aff_ref, thr_ref, ngt_ref, *, cap):
    lg = lg_ref[...]
    e = jnp.exp(lg - jnp.max(lg, axis=0, keepdims=True))
    aff = e / jnp.sum(e, axis=0, keepdims=True)
    aff_ref[...] = aff
    bits = lax.bitcast_convert_type(aff, jnp.int32)

    def step(i, t):
        cand = t | (jnp.int32(1) << (30 - i))
        cnt = jnp.sum((bits >= cand).astype(jnp.int32), axis=1, keepdims=True)
        return jnp.where(cnt >= cap, cand, t)

    thr = lax.fori_loop(0, 31, step, jnp.zeros((lg.shape[0], 1), jnp.int32))
    ngt = jnp.sum((bits > thr).astype(jnp.int32), axis=1, keepdims=True)
    thr_ref[...] = jnp.broadcast_to(thr, thr_ref.shape)
    ngt_ref[...] = jnp.broadcast_to(ngt, ngt_ref.shape)


def _route(logits_t, cap):
    e, n = logits_t.shape
    full = lambda s: pl.BlockSpec(s, lambda: (0, 0))
    return pl.pallas_call(
        functools.partial(_route_kernel, cap=cap),
        in_specs=[full((e, n))],
        out_specs=[full((e, n)), full((e, LANE)), full((e, LANE))],
        out_shape=[jax.ShapeDtypeStruct((e, n), F32), jax.ShapeDtypeStruct((e, LANE), jnp.int32),
                   jax.ShapeDtypeStruct((e, LANE), jnp.int32)],
        compiler_params=_cparams(None),
        name="route_select",
    )(logits_t)


def _select_indices(aff, thr, ngt, cap):
    e, n = aff.shape
    bits = lax.bitcast_convert_type(aff, jnp.int32)
    t, g = thr[:, :1], ngt[:, :1]
    gt, eq = bits > t, bits == t
    sel = gt | (eq & (jnp.cumsum(eq.astype(jnp.int32), axis=1) <= cap - g))
    pos = jnp.cumsum(sel.astype(jnp.int32), axis=1) - 1
    tok = jnp.broadcast_to(jnp.arange(n, dtype=jnp.int32)[None], (e, n))
    rows = jnp.broadcast_to(jnp.arange(e, dtype=jnp.int32)[:, None], (e, n))
    idx = jnp.zeros((e, cap), jnp.int32).at[rows, jnp.where(sel, pos, cap)].set(tok, mode='drop')
    gval = jnp.take_along_axis(aff, idx, axis=1)
    return idx, gval


def _moe_kernel(idx_ref, x_hbm, acc_in, gv_ref, wg_ref, wu_ref, wd_ref, acc_out, xbuf, obuf, xb, hacc,
                sem_x, sem_o, sem_w, *, tm):
    f = pl.program_id(2)

    @pl.when(f == 0)
    def _():
        def issue(r, c):
            t = idx_ref[0, 0, r]
            pltpu.make_async_copy(x_hbm.at[pl.ds(t, 1)], xbuf.at[pl.ds(r, 1)], sem_x).start()
            pltpu.make_async_copy(acc_in.at[pl.ds(t, 1)], obuf.at[pl.ds(r, 1)], sem_o).start()
            return c

        lax.fori_loop(0, tm, issue, 0)
        pltpu.make_async_copy(x_hbm.at[pl.ds(0, tm)], xbuf, sem_x).wait()
        xb[...] = xbuf[...].astype(BF16)
        hacc[...] = jnp.zeros(hacc.shape, F32)

    x = xb[...]
    gate = jnp.dot(x, wg_ref[0], preferred_element_type=F32)
    up = jnp.dot(x, wu_ref[0], preferred_element_type=F32)
    hh = (gate * jax.nn.sigmoid(gate) * up).astype(BF16)
    hacc[...] += jnp.dot(hh, wd_ref[0], preferred_element_type=F32)

    @pl.when(f == pl.num_programs(2) - 1)
    def _():
        pltpu.make_async_copy(acc_in.at[pl.ds(0, tm)], obuf, sem_o).wait()
        obuf[...] = obuf[...] + hacc[...] * gv_ref[0]

        def issue(r, c):
            t = idx_ref[0, 0, r]
            pltpu.make_async_copy(obuf.at[pl.ds(r, 1)], acc_out.at[pl.ds(t, 1)], sem_w).start()
            return c

        lax.fori_loop(0, tm, issue, 0)
        pltpu.make_async_copy(obuf, acc_out.at[pl.ds(0, tm)], sem_w).wait()


def _moe_ffn(x1, acc0, idx, gval, wg, wu, wd, tm, tf):
    n, d = x1.shape
    e, cap = idx.shape
    ff = wg.shape[2]
    tm, tf = min(tm, cap), min(tf, ff)
    nt, nf = cap // tm, ff // tf
    idx3 = idx.reshape(e * nt, 1, tm)
    gv3 = gval.reshape(e * nt, tm, 1)
    anyspec = pl.BlockSpec(memory_space=pl.ANY)
    return pl.pallas_call(
        functools.partial(_moe_kernel, tm=tm),
        grid=(e, nt, nf),
        in_specs=[pl.BlockSpec((1, 1, tm), lambda ei, i, f: (ei * nt + i, 0, 0), memory_space=pltpu.SMEM),
                  anyspec, anyspec,
                  pl.BlockSpec((1, tm, 1), lambda ei, i, f: (ei * nt + i, 0, 0)),
                  pl.BlockSpec((1, d, tf), lambda ei, i, f: (ei, 0, f)),
                  pl.BlockSpec((1, d, tf), lambda ei, i, f: (ei, 0, f)),
                  pl.BlockSpec((1, tf, d), lambda ei, i, f: (ei, f, 0))],
        out_specs=anyspec,
        out_shape=jax.ShapeDtypeStruct((n, d), F32),
        scratch_shapes=[pltpu.VMEM((tm, d), F32), pltpu.VMEM((tm, d), F32), pltpu.VMEM((tm, d), BF16),
                        pltpu.VMEM((tm, d), F32), pltpu.SemaphoreType.DMA(()), pltpu.SemaphoreType.DMA(()),
                        pltpu.SemaphoreType.DMA(())],
        input_output_aliases={2: 0},
        compiler_params=_cparams(("arbitrary", "arbitrary", "arbitrary")),
        name="moe_ffn",
    )(idx3, x1, acc0, gv3, wg, wu, wd)


def _ln_kernel(z_ref, g_ref, b_ref, x_ref, xb_ref):
    x = _layer_norm(z_ref[...], g_ref[...], b_ref[...])
    x_ref[...] = x
    xb_ref[...] = x.astype(BF16)


def _ln_out(z, g, b, tm):
    n, d = z.shape
    tm = min(tm, n)
    row = pl.BlockSpec((tm, d), lambda i: (i, 0))
    vec = pl.BlockSpec((1, d), lambda i: (0, 0))
    return pl.pallas_call(
        _ln_kernel,
        grid=(n // tm,),
        in_specs=[row, vec, vec],
        out_specs=[row, row],
        out_shape=[jax.ShapeDtypeStruct((n, d), F32), jax.ShapeDtypeStruct((n, d), BF16)],
        compiler_params=_cparams(("parallel",)),
        name="ln_out",
    )(z, g, b)


def _t5_bucket(rel):
    nb = N_BUCKETS // 2
    max_exact = nb // 2
    n = jnp.abs(rel)
    scaled = jnp.log(jnp.maximum(n, 1).astype(F32) / max_exact) / math.log(MAX_DIST / max_exact)
    large = jnp.minimum(max_exact + (scaled * (nb - max_exact)).astype(jnp.int32), nb - 1)
    return jnp.where(rel > 0, nb, 0) + jnp.where(n < max_exact, n, large)


def _rope_tabs(ang_first, ang_second):
    def half(ang):
        if ang is None:
            return None
        return jnp.concatenate([ang, ang], axis=-1)
    a1, a2 = half(ang_first), half(ang_second)
    s = a1.shape[0]
    if a2 is None:
        cos = jnp.concatenate([jnp.cos(a1), jnp.ones((s, 64), F32)], -1)
        sin = jnp.concatenate([jnp.sin(a1), jnp.zeros((s, 64), F32)], -1)
    else:
        cos = jnp.concatenate([jnp.cos(a1), jnp.cos(a2)], -1)
        sin = jnp.concatenate([jnp.sin(a1), jnp.sin(a2)], -1)
    first = (jnp.arange(LANE) % 64) < 32
    return cos, jnp.where(first, -sin, 0.0), jnp.where(first, 0.0, sin)


def _tables(seq):
    rows = seq // GRID_W
    row = jnp.repeat(jnp.arange(rows), GRID_W).astype(F32)
    col = jnp.tile(jnp.arange(GRID_W), rows).astype(F32)
    t = jnp.arange(seq).astype(F32)
    inv_ax = ROPE_THETA ** (-jnp.arange(0, HEAD_DIM // 2, 2, dtype=F32) / (HEAD_DIM // 2))
    inv_t = ROPE_THETA ** (-jnp.arange(0, C_ROPE, 2, dtype=F32) / C_ROPE)
    ax = _rope_tabs(row[:, None] * inv_ax[None], col[:, None] * inv_ax[None])
    tt = _rope_tabs(t[:, None] * inv_t[None], None)
    return ax, tt


def _prep_weights(p):
    off, total = _layout()
    d = D_MODEL
    L = p['w_in'].shape[0]
    splits = np.cumsum([A_HEADS * HEAD_DIM, A_KV_HEADS * HEAD_DIM, A_KV_HEADS * HEAD_DIM, B_HEADS * HEAD_DIM,
                        B_KV_HEADS * HEAD_DIM, B_KV_HEADS * HEAD_DIM, C_Q_RANK, C_KV_RANK, C_ROPE])
    qa, ka, va, qb, kb, vb, cq, ckv, kr, g = jnp.split(p['w_in'].astype(BF16), [int(s) for s in splits], axis=-1)
    z = lambda w: jnp.zeros((L, d, w), BF16)
    krp = jnp.concatenate([z(LANE), kr, z(C_PAD - LANE - C_ROPE)], -1)
    cols = dict(g=g, qa=qa, qb=qb, cq=cq, ckv=ckv, ka=ka, va=va, kb=kb, vb=vb, kr=krp)
    parts, o = [], 0
    for name, (start, width) in off.items():
        assert start == o and cols[name].shape[-1] == width
        parts.append(cols[name])
        o += width
    parts.append(z(total - o))
    w_in = jnp.concatenate(parts, -1)

    wuq = p['w_uq'].astype(BF16).reshape(L, C_Q_RANK, C_HEADS, C_NOPE + C_ROPE)
    wuq = jnp.pad(wuq, ((0, 0), (0, 0), (0, 0), (0, C_PAD - C_NOPE - C_ROPE))).reshape(L, C_Q_RANK, C_HEADS * C_PAD)
    wukv = p['w_ukv'].astype(BF16).reshape(L, C_KV_RANK, C_HEADS, C_NOPE + C_V)
    wk = wukv[..., :C_NOPE].reshape(L, C_KV_RANK, C_HEADS * C_NOPE)
    wv = wukv[..., C_NOPE:].reshape(L, C_KV_RANK, C_HEADS * C_V)
    wr = jnp.pad(p['w_router'], ((0, 0), (0, 0), (0, LANE - N_EXPERTS)))
    row = lambda a: a[:, None, :]
    g_heads = A_HEADS // A_KV_HEADS
    sink = jnp.broadcast_to(p['a_sink'].reshape(L, A_KV_HEADS, g_heads, 1, 1),
                            (L, A_KV_HEADS, g_heads, BLOCK, 1)).reshape(L, A_KV_HEADS, g_heads * BLOCK, 1)
    return dict(w_in=w_in, b_gate=row(p['b_gate']), gqb=row(p['b_q_norm']), gkb=row(p['b_k_norm']),
                gcq=row(p['c_q_norm']), gckv=row(p['c_kv_norm']), wuq=wuq, wk=wk, wv=wv, sink=sink,
                wa=p['w_branch_a'].astype(BF16), wb=p['w_branch_b'].astype(BF16), wc=p['w_branch_c'].astype(BF16),
                wo=p['w_o'].astype(BF16), ln1_g=row(p['ln1_g']), ln1_b=row(p['ln1_b']), wr=wr,
                weg=p['w_e_gate'].astype(BF16), weu=p['w_e_up'].astype(BF16), wed=p['w_e_down'].astype(BF16),
                ln2_g=row(p['ln2_g']), ln2_b=row(p['ln2_b']))


def _layer(x, xb, w, bias, tabs, bsz, seq):
    off, _ = _layout()
    n = bsz * seq
    ax_tabs, t_tabs = tabs
    h = _matmul(xb, w['w_in'], BF16, 2048, 512)
    oa = _window_attention(h, bias, w['sink'], bsz, seq)
    qb, kb = _b_prep(h, ax_tabs, w['gqb'], w['gkb'], seq, 512)
    ob = _flash(qb, kb, h, bsz, seq, B_HEADS, B_KV_HEADS, HEAD_DIM, HEAD_DIM, 0, 0, off['vb'][0] // HEAD_DIM,
                512, 1024)
    qc, kc, vc = _c_prep(h, w['wuq'], w['wk'], w['wv'], w['gcq'], w['gckv'], t_tabs, seq, 512)
    oc = _flash(qc, kc, vc, bsz, seq, C_HEADS, C_HEADS, C_PAD, C_V, 0, 0, 0, 512, 1024)
    merged = _merge(oa, ob, oc, w['wa'], w['wb'], w['wc'], h, w['b_gate'], 1024, 1024)
    x1, acc0, logits = _wo_ln(merged, w['wo'], x, w['ln1_g'], w['ln1_b'], w['wr'], 512)
    cap = EC_FACTOR * n // N_EXPERTS
    aff, thr, ngt = _route(logits[:, :N_EXPERTS].T, cap)
    idx, gval = _select_indices(aff, thr, ngt, cap)
    acc = _moe_ffn(x1, acc0, idx, gval, w['weg'], w['weu'], w['wed'], 512, 512)
    return _ln_out(acc, w['ln2_g'], w['ln2_b'], 512)


def _rel_bias(rel_table):
    offs = jnp.arange(3 * BLOCK)[None, :] - BLOCK - jnp.arange(BLOCK)[:, None]
    rb = jnp.transpose(rel_table[_t5_bucket(offs)], (2, 0, 1)).astype(F32)
    g = A_HEADS // A_KV_HEADS
    return rb.reshape(A_KV_HEADS, g * BLOCK, 3 * BLOCK)


def kernel(x_prompt, x_sample, rel_table, w_in, b_gate, b_q_norm, b_k_norm, c_q_norm, c_kv_norm, w_uq, w_ukv, a_sink,
           w_branch_a, w_branch_b, w_branch_c, w_o, ln1_g, ln1_b, w_router, w_e_gate, w_e_up, w_e_down, ln2_g, ln2_b):
    params = dict(w_in=w_in, b_gate=b_gate, b_q_norm=b_q_norm, b_k_norm=b_k_norm, c_q_norm=c_q_norm,
                  c_kv_norm=c_kv_norm, w_uq=w_uq, w_ukv=w_ukv, a_sink=a_sink, w_branch_a=w_branch_a,
                  w_branch_b=w_branch_b, w_branch_c=w_branch_c, w_o=w_o, ln1_g=ln1_g, ln1_b=ln1_b,
                  w_router=w_router, w_e_gate=w_e_gate, w_e_up=w_e_up, w_e_down=w_e_down, ln2_g=ln2_g, ln2_b=ln2_b)
    w_all = _prep_weights(params)
    bias = _rel_bias(rel_table)
    groups = []
    for x in (x_prompt, x_sample):
        bsz, seq, d = x.shape
        groups.append((bsz, seq, _tables(seq)))
    xs = tuple(x.reshape(-1, x.shape[-1]) for x in (x_prompt, x_sample))
    carry = tuple((x, x.astype(BF16)) for x in xs)

    def body(carry, w):
        out = []
        for (x, xb), (bsz, seq, tabs) in zip(carry, groups):
            out.append(_layer(x, xb, w, bias, tabs, bsz, seq))
        return tuple(out), None

    carry, _ = lax.scan(body, carry, w_all)
    return tuple(c[0].reshape(xin.shape) for c, xin in zip(carry, (x_prompt, x_sample)))
```

```python
import functools
import math

import jax
import jax.numpy as jnp
import numpy as np
from jax import lax
from jax.experimental import pallas as pl
from jax.experimental.pallas import tpu as pltpu

D_MODEL = 2048
DEPTH = 4
GRID_W = 64
BLOCK = 128
HEAD_DIM = 128
A_HEADS = 8
A_KV_HEADS = 2
WINDOW = 128
B_HEADS = 8
B_KV_HEADS = 2
C_HEADS = 8
C_Q_RANK = 512
C_KV_RANK = 512
C_NOPE = 128
C_ROPE = 64
C_V = 128
ROPE_THETA = 10000.0
N_BUCKETS = 32
MAX_DIST = 128
N_EXPERTS = 16
EXPERT_FF = 2048
EC_FACTOR = 2
N_BRANCH = 3
EPS = 1e-6
NEG = -1e30
LOG2E = 1.4426950408889634

LANE = 128
C_PAD = 2 * LANE
VMEM_LIMIT = 56 * 1024 * 1024

BF16 = jnp.bfloat16
F32 = jnp.float32


def _dn_alpha():
    return (2 * DEPTH) ** 0.25


def _layout():
    d = D_MODEL
    segs = [('g', N_BRANCH * d), ('qa', A_HEADS * HEAD_DIM), ('qb', B_HEADS * HEAD_DIM), ('cq', C_Q_RANK),
            ('ckv', C_KV_RANK), ('ka', A_KV_HEADS * HEAD_DIM), ('va', A_KV_HEADS * HEAD_DIM),
            ('kb', B_KV_HEADS * HEAD_DIM), ('vb', B_KV_HEADS * HEAD_DIM), ('kr', C_PAD)]
    off, o = {}, 0
    for name, w in segs:
        off[name] = (o, w)
        o += w
    total = -(-o // 512) * 512
    return off, total


def _cparams(sem, vmem=VMEM_LIMIT):
    return pltpu.CompilerParams(dimension_semantics=sem, vmem_limit_bytes=vmem)


def _mm_kernel(x_ref, w_ref, o_ref):
    o_ref[...] = jnp.dot(x_ref[...], w_ref[...], preferred_element_type=F32).astype(o_ref.dtype)


def _matmul(x, w, out_dtype, tm, tn):
    m, k = x.shape
    n = w.shape[1]
    tm, tn = min(tm, m), min(tn, n)
    return pl.pallas_call(
        _mm_kernel,
        grid=(m // tm, n // tn),
        in_specs=[pl.BlockSpec((tm, k), lambda i, j: (i, 0)), pl.BlockSpec((k, tn), lambda i, j: (0, j))],
        out_specs=pl.BlockSpec((tm, tn), lambda i, j: (i, j)),
        out_shape=jax.ShapeDtypeStruct((m, n), out_dtype),
        compiler_params=_cparams(("parallel", "parallel")),
        name="in_proj",
    )(x, w)


def _rope(x, cos, sa, sb):
    return x * cos + pltpu.roll(x, LANE - 32, 1) * sa + pltpu.roll(x, 32, 1) * sb


def _rms(x, g):
    return x * lax.rsqrt(jnp.mean(x * x, axis=-1, keepdims=True) + EPS) * g


def _bprep_kernel(q_ref, k_ref, cos_ref, sa_ref, sb_ref, gq_ref, gk_ref, qo_ref, ko_ref, *, scale):
    cos, sa, sb = cos_ref[...], sa_ref[...], sb_ref[...]
    gq, gk = gq_ref[...], gk_ref[...]
    for h in range(B_HEADS):
        x = q_ref[:, h * LANE:(h + 1) * LANE].astype(F32)
        qo_ref[:, h * LANE:(h + 1) * LANE] = (_rope(_rms(x, gq), cos, sa, sb) * scale).astype(BF16)
    for h in range(B_KV_HEADS):
        x = k_ref[:, h * LANE:(h + 1) * LANE].astype(F32)
        ko_ref[:, h * LANE:(h + 1) * LANE] = _rope(_rms(x, gk), cos, sa, sb).astype(BF16)


def _b_prep(h, tabs, gq, gk, seq, tm):
    off, _ = _layout()
    n = h.shape[0]
    tm = min(tm, seq)
    spb = seq // tm
    qw, kw = B_HEADS * HEAD_DIM, B_KV_HEADS * HEAD_DIM
    qo, ko = off['qb'][0] // qw, off['kb'][0] // kw
    tab_spec = pl.BlockSpec((tm, LANE), lambda i: (i % spb, 0))
    vec_spec = pl.BlockSpec((1, LANE), lambda i: (0, 0))
    scale = HEAD_DIM ** -0.5 * LOG2E
    return pl.pallas_call(
        functools.partial(_bprep_kernel, scale=scale),
        grid=(n // tm,),
        in_specs=[pl.BlockSpec((tm, qw), lambda i: (i, qo)), pl.BlockSpec((tm, kw), lambda i: (i, ko)),
                  tab_spec, tab_spec, tab_spec, vec_spec, vec_spec],
        out_specs=[pl.BlockSpec((tm, qw), lambda i: (i, 0)), pl.BlockSpec((tm, kw), lambda i: (i, 0))],
        out_shape=[jax.ShapeDtypeStruct((n, qw), BF16), jax.ShapeDtypeStruct((n, kw), BF16)],
        compiler_params=_cparams(("parallel",)),
        name="b_prep",
    )(h, h, *tabs, gq, gk)


def _cprep_kernel(cq_ref, ckv_ref, kr_ref, wq_ref, wk_ref, wv_ref, gq_ref, gkv_ref, cos_ref, sa_ref, sb_ref,
                  qo_ref, ko_ref, vo_ref, *, scale):
    cos, sa, sb = cos_ref[...], sa_ref[...], sb_ref[...]
    cqn = _rms(cq_ref[...].astype(F32), gq_ref[...]).astype(BF16)
    ckvn = _rms(ckv_ref[...].astype(F32), gkv_ref[...]).astype(BF16)
    q = jnp.dot(cqn, wq_ref[...], preferred_element_type=F32)
    kn = jnp.dot(ckvn, wk_ref[...], preferred_element_type=F32)
    vo_ref[...] = jnp.dot(ckvn, wv_ref[...], preferred_element_type=F32).astype(BF16)
    kr = _rope(kr_ref[:, LANE:].astype(F32), cos, sa, sb).astype(BF16)
    for h in range(C_HEADS):
        b = h * C_PAD
        qo_ref[:, b:b + LANE] = (q[:, b:b + LANE] * scale).astype(BF16)
        qo_ref[:, b + LANE:b + C_PAD] = (_rope(q[:, b + LANE:b + C_PAD], cos, sa, sb) * scale).astype(BF16)
        ko_ref[:, b:b + LANE] = kn[:, h * LANE:(h + 1) * LANE].astype(BF16)
        ko_ref[:, b + LANE:b + C_PAD] = kr


def _c_prep(h, wq, wk, wv, gq, gkv, tabs, seq, tm):
    off, _ = _layout()
    n = h.shape[0]
    tm = min(tm, seq)
    spb = seq // tm
    cqo, ckvo, kro = off['cq'][0] // C_Q_RANK, off['ckv'][0] // C_KV_RANK, off['kr'][0] // C_PAD
    tab_spec = pl.BlockSpec((tm, LANE), lambda i: (i % spb, 0))
    full = lambda a: pl.BlockSpec(a.shape, lambda i: (0, 0))
    scale = (C_NOPE + C_ROPE) ** -0.5 * LOG2E
    qw, vw = C_HEADS * C_PAD, C_HEADS * C_V
    return pl.pallas_call(
        functools.partial(_cprep_kernel, scale=scale),
        grid=(n // tm,),
        in_specs=[pl.BlockSpec((tm, C_Q_RANK), lambda i: (i, cqo)), pl.BlockSpec((tm, C_KV_RANK), lambda i: (i, ckvo)),
                  pl.BlockSpec((tm, C_PAD), lambda i: (i, kro)), full(wq), full(wk), full(wv), full(gq), full(gkv),
                  tab_spec, tab_spec, tab_spec],
        out_specs=[pl.BlockSpec((tm, qw), lambda i: (i, 0)), pl.BlockSpec((tm, qw), lambda i: (i, 0)),
                   pl.BlockSpec((tm, vw), lambda i: (i, 0))],
        out_shape=[jax.ShapeDtypeStruct((n, qw), BF16), jax.ShapeDtypeStruct((n, qw), BF16),
                   jax.ShapeDtypeStruct((n, vw), BF16)],
        compiler_params=_cparams(("parallel",)),
        name="c_prep",
    )(h, h, h, wq, wk, wv, gq, gkv, *tabs)


def _win_kernel(q_ref, kp_ref, kc_ref, kn_ref, vp_ref, vc_ref, vn_ref, bias_ref, sink_ref, o_ref, *, seq, scale):
    nblk = pl.program_id(1)
    g = A_HEADS // A_KV_HEADS
    rows = g * BLOCK
    qi = lax.broadcasted_iota(jnp.int32, (rows, 3 * BLOCK), 0) & (BLOCK - 1)
    ci = lax.broadcasted_iota(jnp.int32, (rows, 3 * BLOCK), 1)
    offs = ci - BLOCK - qi
    key_pos = nblk * BLOCK - BLOCK + ci
    valid = (jnp.abs(offs) <= WINDOW) & (key_pos >= 0) & (key_pos < seq)
    for hk in range(A_KV_HEADS):
        ks = slice(hk * HEAD_DIM, (hk + 1) * HEAD_DIM)
        k = jnp.concatenate([kp_ref[:, ks], kc_ref[:, ks], kn_ref[:, ks]], axis=0)
        v = jnp.concatenate([vp_ref[:, ks], vc_ref[:, ks], vn_ref[:, ks]], axis=0)
        q = jnp.concatenate([q_ref[:, (hk * g + j) * HEAD_DIM:(hk * g + j + 1) * HEAD_DIM] for j in range(g)],
                            axis=0)
        s = lax.dot_general(q, k, (((1,), (1,)), ((), ())), preferred_element_type=F32) * scale + bias_ref[hk]
        s = jnp.where(valid, s, NEG)
        sink = sink_ref[hk]
        m = jnp.maximum(jnp.max(s, axis=-1, keepdims=True), sink)
        p = jnp.exp(s - m)
        denom = jnp.sum(p, axis=-1, keepdims=True) + jnp.exp(sink - m)
        o = jnp.dot(p.astype(BF16), v, preferred_element_type=F32) / denom
        for j in range(g):
            o_ref[:, (hk * g + j) * HEAD_DIM:(hk * g + j + 1) * HEAD_DIM] = o[j * BLOCK:(j + 1) * BLOCK].astype(BF16)


def _window_attention(h, bias, sink, bsz, seq):
    off, _ = _layout()
    nb = seq // BLOCK
    qw, kw = A_HEADS * HEAD_DIM, A_KV_HEADS * HEAD_DIM
    qo, ko, vo = off['qa'][0] // qw, off['ka'][0] // kw, off['va'][0] // kw
    cur = lambda b, n: b * nb + n
    prv = lambda b, n: b * nb + jnp.maximum(n - 1, 0)
    nxt = lambda b, n: b * nb + jnp.minimum(n + 1, nb - 1)
    kv = lambda f, c: pl.BlockSpec((BLOCK, kw), lambda b, n: (f(b, n), c))
    full = lambda a: pl.BlockSpec(a.shape, lambda b, n: (0,) * a.ndim)
    return pl.pallas_call(
        functools.partial(_win_kernel, seq=seq, scale=HEAD_DIM ** -0.5),
        grid=(bsz, nb),
        in_specs=[pl.BlockSpec((BLOCK, qw), lambda b, n: (cur(b, n), qo)),
                  kv(prv, ko), kv(cur, ko), kv(nxt, ko), kv(prv, vo), kv(cur, vo), kv(nxt, vo),
                  full(bias), full(sink)],
        out_specs=pl.BlockSpec((BLOCK, qw), lambda b, n: (cur(b, n), 0)),
        out_shape=jax.ShapeDtypeStruct((bsz * seq, qw), BF16),
        compiler_params=_cparams(("parallel", "parallel")),
        name="window_attn",
    )(h, h, h, h, h, h, h, bias, sink)


Q_CHUNK = 2 * LANE
KV_SUB = 512


def _flash_kernel(q_ref, k_ref, vt_ref, o_ref, m_sc, l_sc, acc_sc, s0_sc, s1_sc, *, nsub, nchunk, g, dq, dv):
    ki = pl.program_id(3)

    @pl.when(ki == 0)
    def _():
        m_sc[...] = jnp.full(m_sc.shape, -jnp.inf, F32)
        l_sc[...] = jnp.zeros(l_sc.shape, F32)
        acc_sc[...] = jnp.zeros(acc_sc.shape, F32)

    def q_chunk(j):
        return q_ref[:, j * dq:(j + 1) * dq] if g > 1 else q_ref[j * Q_CHUNK:(j + 1) * Q_CHUNK, :]

    def tree(x, op):
        acc = x[:8]
        for r in range(8, x.shape[0], 8):
            acc = op(acc, x[r:r + 8])
        while acc.shape[0] > 1:
            half = acc.shape[0] // 2
            acc = op(acc[:half], acc[half:])
        return acc[0]

    def scores(t, s_ref):
        k = k_ref[pl.ds(pl.multiple_of(t * KV_SUB, KV_SUB), KV_SUB), :]
        for j in range(nchunk):
            s_ref[:, j * Q_CHUNK:(j + 1) * Q_CHUNK] = lax.dot_general(
                k, q_chunk(j), (((1,), (1,)), ((), ())), preferred_element_type=F32)

    def softmax_pv(t, s_ref):
        vt = vt_ref[0, 0, t]
        for j in range(nchunk):
            cs = slice(j * Q_CHUNK, (j + 1) * Q_CHUNK)
            st = s_ref[:, cs]
            m_prev = m_sc[:, cs]
            m_cur = jnp.max(tree(st.reshape(KV_SUB // 8, 8, Q_CHUNK), jnp.maximum), axis=0, keepdims=True)
            m_new = jnp.maximum(m_prev, m_cur)
            alpha = jnp.exp2(m_prev - m_new)
            p = jnp.exp2(st - m_new)
            l_cur = jnp.sum(tree(p.reshape(KV_SUB // 8, 8, Q_CHUNK), jnp.add), axis=0, keepdims=True)
            l_sc[:, cs] = alpha * l_sc[:, cs] + l_cur
            acc_sc[:, cs] = alpha * acc_sc[:, cs] + jnp.dot(vt, p.astype(BF16), preferred_element_type=F32)
            m_sc[:, cs] = m_new

    scores(0, s0_sc)

    def body(i, carry):
        t = 2 * i
        scores(t + 1, s1_sc)
        softmax_pv(t, s0_sc)
        scores(t + 2, s0_sc)
        softmax_pv(t + 1, s1_sc)
        return carry

    lax.fori_loop(0, nsub // 2 - 1, body, 0)
    scores(nsub - 1, s1_sc)
    softmax_pv(nsub - 2, s0_sc)
    softmax_pv(nsub - 1, s1_sc)

    @pl.when(ki == pl.num_programs(3) - 1)
    def _():
        for j in range(nchunk):
            cs = slice(j * Q_CHUNK, (j + 1) * Q_CHUNK)
            o = (acc_sc[:, cs] / l_sc[:, cs]).T.astype(o_ref.dtype)
            if g > 1:
                o_ref[:, j * dv:(j + 1) * dv] = o
            else:
                o_ref[j * Q_CHUNK:(j + 1) * Q_CHUNK, :] = o


def _kv_major_t(v, bsz, seq, hk, dv):
    sub = min(KV_SUB, seq)
    return v.reshape(bsz, seq // sub, sub, hk, dv).transpose(0, 3, 1, 4, 2)


def _flash(q, k, vt, bsz, seq, hq, hk, dq, dv, bk):
    g = hq // hk
    assert seq % KV_SUB == 0 and (g == 1 or g * Q_CHUNK <= 4 * Q_CHUNK)
    bk = min(bk, seq)
    nchunk = g if g > 1 else 4
    bq = Q_CHUNK if g > 1 else nchunk * Q_CHUNK
    assert seq % bq == 0 and seq % bk == 0 and bk % (2 * KV_SUB) == 0
    nq, nk, nsub = seq // bq, seq // bk, bk // KV_SUB
    return pl.pallas_call(
        functools.partial(_flash_kernel, nsub=nsub, nchunk=nchunk, g=g, dq=dq, dv=dv),
        grid=(bsz, hk, nq, nk),
        in_specs=[pl.BlockSpec((bq, g * dq), lambda b, h, i, j: (b * nq + i, h)),
                  pl.BlockSpec((bk, dq), lambda b, h, i, j: (b * nk + j, h)),
                  pl.BlockSpec((1, 1, nsub, dv, KV_SUB), lambda b, h, i, j: (b, h, j, 0, 0))],
        out_specs=pl.BlockSpec((bq, g * dv), lambda b, h, i, j: (b * nq + i, h)),
        out_shape=jax.ShapeDtypeStruct((bsz * seq, hq * dv), BF16),
        scratch_shapes=[pltpu.VMEM((1, nchunk * Q_CHUNK), F32), pltpu.VMEM((1, nchunk * Q_CHUNK), F32),
                        pltpu.VMEM((dv, nchunk * Q_CHUNK), F32), pltpu.VMEM((KV_SUB, nchunk * Q_CHUNK), F32),
                        pltpu.VMEM((KV_SUB, nchunk * Q_CHUNK), F32)],
        compiler_params=_cparams(("parallel", "parallel", "parallel", "arbitrary")),
        name="dense_attn",
    )(q, k, vt)


def _merge_kernel(oa_ref, ob_ref, oc_ref, wa_ref, wb_ref, wc_ref, ga_ref, gb_ref, gc_ref, ba_ref, bb_ref, bc_ref,
                  o_ref):
    def branch(o_r, w_r, g_r, b_r):
        gate = jax.nn.sigmoid(g_r[...].astype(F32) + b_r[...])
        return gate * jnp.dot(o_r[...], w_r[...], preferred_element_type=F32)

    o_ref[...] = (branch(oa_ref, wa_ref, ga_ref, ba_ref) + branch(ob_ref, wb_ref, gb_ref, bb_ref)
                  + branch(oc_ref, wc_ref, gc_ref, bc_ref)).astype(o_ref.dtype)


def _merge(oa, ob, oc, wa, wb, wc, h, b_gate, tm, tn):
    n, kdim = oa.shape
    d = D_MODEL
    tm, tn = min(tm, n), min(tn, d)
    nj = d // tn
    act = pl.BlockSpec((tm, kdim), lambda i, j: (i, 0))
    wsp = pl.BlockSpec((kdim, tn), lambda i, j: (0, j))
    gate = lambda br: pl.BlockSpec((tm, tn), lambda i, j: (i, br * nj + j))
    bias = lambda br: pl.BlockSpec((1, tn), lambda i, j: (0, br * nj + j))
    return pl.pallas_call(
        _merge_kernel,
        grid=(n // tm, nj),
        in_specs=[act, act, act, wsp, wsp, wsp, gate(0), gate(1), gate(2), bias(0), bias(1), bias(2)],
        out_specs=pl.BlockSpec((tm, tn), lambda i, j: (i, j)),
        out_shape=jax.ShapeDtypeStruct((n, d), BF16),
        compiler_params=_cparams(("parallel", "parallel")),
        name="branch_merge",
    )(oa, ob, oc, wa, wb, wc, h, h, h, b_gate, b_gate, b_gate)


def _layer_norm(z, g, b):
    zc = z - jnp.mean(z, axis=-1, keepdims=True)
    var = jnp.mean(zc * zc, axis=-1, keepdims=True)
    return zc * lax.rsqrt(var + EPS) * g + b


def _woln_kernel(m_ref, wo_ref, x_ref, g_ref, b_ref, wr_ref, x1_ref, acc_ref, lg_ref, *, alpha):
    y = jnp.dot(m_ref[...], wo_ref[...], preferred_element_type=F32)
    x1 = _layer_norm(alpha * x_ref[...] + y, g_ref[...], b_ref[...])
    x1_ref[...] = x1
    acc_ref[...] = alpha * x1
    lg_ref[...] = jnp.dot(x1.astype(BF16), wr_ref[...], preferred_element_type=F32)


def _wo_ln(merged, wo, x, g, b, wr, tm):
    n, d = x.shape
    tm = min(tm, n)
    row = lambda w: pl.BlockSpec((tm, w), lambda i: (i, 0))
    full = lambda a: pl.BlockSpec(a.shape, lambda i: (0, 0))
    return pl.pallas_call(
        functools.partial(_woln_kernel, alpha=_dn_alpha()),
        grid=(n // tm,),
        in_specs=[row(d), full(wo), row(d), full(g), full(b), full(wr)],
        out_specs=[row(d), row(d), row(LANE)],
        out_shape=[jax.ShapeDtypeStruct((n, d), F32), jax.ShapeDtypeStruct((n, d), F32),
                   jax.ShapeDtypeStruct((n, LANE), F32)],
        compiler_params=_cparams(("parallel",)),
        name="wo_ln_router",
    )(merged, wo, x, g, b, wr)


def _route_kernel(lg_ref, aff_ref, thr_ref, ngt_ref, *, cap):
    lg = lg_ref[...]
    e = jnp.exp(lg - jnp.max(lg, axis=0, keepdims=True))
    aff = e / jnp.sum(e, axis=0, keepdims=True)
    aff_ref[...] = aff
    bits = lax.bitcast_convert_type(aff, jnp.int32)

    def step(i, t):
        cand = t | (jnp.int32(1) << (30 - i))
        cnt = jnp.sum((bits >= cand).astype(jnp.int32), axis=1, keepdims=True)
        return jnp.where(cnt >= cap, cand, t)

    thr = lax.fori_loop(0, 31, step, jnp.zeros((lg.shape[0], 1), jnp.int32))
    ngt = jnp.sum((bits > thr).astype(jnp.int32), axis=1, keepdims=True)
    thr_ref[...] = jnp.broadcast_to(thr, thr_ref.shape)
    ngt_ref[...] = jnp.broadcast_to(ngt, ngt_ref.shape)


def _route(logits_t, cap):
    e, n = logits_t.shape
    full = lambda s: pl.BlockSpec(s, lambda: (0, 0))
    return pl.pallas_call(
        functools.partial(_route_kernel, cap=cap),
        in_specs=[full((e, n))],
        out_specs=[full((e, n)), full((e, LANE)), full((e, LANE))],
        out_shape=[jax.ShapeDtypeStruct((e, n), F32), jax.ShapeDtypeStruct((e, LANE), jnp.int32),
                   jax.ShapeDtypeStruct((e, LANE), jnp.int32)],
        compiler_params=_cparams(None),
        name="route_select",
    )(logits_t)


def _select_indices(aff, thr, ngt, cap):
    e, n = aff.shape
    bits = lax.bitcast_convert_type(aff, jnp.int32)
    t, g = thr[:, :1], ngt[:, :1]
    gt, eq = bits > t, bits == t
    sel = gt | (eq & (jnp.cumsum(eq.astype(jnp.int32), axis=1) <= cap - g))
    tok = jnp.arange(n, dtype=jnp.int32)[None]
    idx, gval = lax.sort((jnp.where(sel, tok, tok + n), aff), dimension=1, num_keys=1)
    return idx[:, :cap], gval[:, :cap]


def _moe_kernel(idx_ref, x_hbm, acc_in, gv_ref, wg_ref, wu_ref, wd_ref, acc_out, xbuf, obuf, xb, hacc,
                sem_x, sem_o, sem_w, *, tm):
    f = pl.program_id(2)

    @pl.when(f == 0)
    def _():
        def issue(r, c):
            t = idx_ref[0, 0, r]
            pltpu.make_async_copy(x_hbm.at[pl.ds(t, 1)], xbuf.at[pl.ds(r, 1)], sem_x).start()
            pltpu.make_async_copy(acc_in.at[pl.ds(t, 1)], obuf.at[pl.ds(r, 1)], sem_o).start()
            return c

        lax.fori_loop(0, tm, issue, 0)
        pltpu.make_async_copy(x_hbm.at[pl.ds(0, tm)], xbuf, sem_x).wait()
        xb[...] = xbuf[...].astype(BF16)
        hacc[...] = jnp.zeros(hacc.shape, F32)

    x = xb[...]
    gate = jnp.dot(x, wg_ref[0], preferred_element_type=F32)
    up = jnp.dot(x, wu_ref[0], preferred_element_type=F32)
    hh = (gate * jax.nn.sigmoid(gate) * up).astype(BF16)
    hacc[...] += jnp.dot(hh, wd_ref[0], preferred_element_type=F32)

    @pl.when(f == pl.num_programs(2) - 1)
    def _():
        pltpu.make_async_copy(acc_in.at[pl.ds(0, tm)], obuf, sem_o).wait()
        obuf[...] = obuf[...] + hacc[...] * gv_ref[0]

        def issue(r, c):
            t = idx_ref[0, 0, r]
            pltpu.make_async_copy(obuf.at[pl.ds(r, 1)], acc_out.at[pl.ds(t, 1)], sem_w).start()
            return c

        lax.fori_loop(0, tm, issue, 0)
        pltpu.make_async_copy(obuf, acc_out.at[pl.ds(0, tm)], sem_w).wait()


def _moe_ffn(x1, acc0, idx, gval, wg, wu, wd, tm, tf):
    n, d = x1.shape
    e, cap = idx.shape
    ff = wg.shape[2]
    tm, tf = min(tm, cap), min(tf, ff)
    nt, nf = cap // tm, ff // tf
    idx3 = idx.reshape(e * nt, 1, tm)
    gv3 = gval.reshape(e * nt, tm, 1)
    anyspec = pl.BlockSpec(memory_space=pl.ANY)
    return pl.pallas_call(
        functools.partial(_moe_kernel, tm=tm),
        grid=(e, nt, nf),
        in_specs=[pl.BlockSpec((1, 1, tm), lambda ei, i, f: (ei * nt + i, 0, 0), memory_space=pltpu.SMEM),
                  anyspec, anyspec,
                  pl.BlockSpec((1, tm, 1), lambda ei, i, f: (ei * nt + i, 0, 0)),
                  pl.BlockSpec((1, d, tf), lambda ei, i, f: (ei, 0, f)),
                  pl.BlockSpec((1, d, tf), lambda ei, i, f: (ei, 0, f)),
                  pl.BlockSpec((1, tf, d), lambda ei, i, f: (ei, f, 0))],
        out_specs=anyspec,
        out_shape=jax.ShapeDtypeStruct((n, d), F32),
        scratch_shapes=[pltpu.VMEM((tm, d), F32), pltpu.VMEM((tm, d), F32), pltpu.VMEM((tm, d), BF16),
                        pltpu.VMEM((tm, d), F32), pltpu.SemaphoreType.DMA(()), pltpu.SemaphoreType.DMA(()),
                        pltpu.SemaphoreType.DMA(())],
        input_output_aliases={2: 0},
        compiler_params=_cparams(("arbitrary", "arbitrary", "arbitrary")),
        name="moe_ffn",
    )(idx3, x1, acc0, gv3, wg, wu, wd)


def _ln_kernel(z_ref, g_ref, b_ref, x_ref, xb_ref):
    x = _layer_norm(z_ref[...], g_ref[...], b_ref[...])
    x_ref[...] = x
    xb_ref[...] = x.astype(BF16)


def _ln_out(z, g, b, tm):
    n, d = z.shape
    tm = min(tm, n)
    row = pl.BlockSpec((tm, d), lambda i: (i, 0))
    vec = pl.BlockSpec((1, d), lambda i: (0, 0))
    return pl.pallas_call(
        _ln_kernel,
        grid=(n // tm,),
        in_specs=[row, vec, vec],
        out_specs=[row, row],
        out_shape=[jax.ShapeDtypeStruct((n, d), F32), jax.ShapeDtypeStruct((n, d), BF16)],
        compiler_params=_cparams(("parallel",)),
        name="ln_out",
    )(z, g, b)


def _t5_bucket(rel):
    nb = N_BUCKETS // 2
    max_exact = nb // 2
    n = jnp.abs(rel)
    scaled = jnp.log(jnp.maximum(n, 1).astype(F32) / max_exact) / math.log(MAX_DIST / max_exact)
    large = jnp.minimum(max_exact + (scaled * (nb - max_exact)).astype(jnp.int32), nb - 1)
    return jnp.where(rel > 0, nb, 0) + jnp.where(n < max_exact, n, large)


def _rope_tabs(ang_first, ang_second):
    def half(ang):
        if ang is None:
            return None
        return jnp.concatenate([ang, ang], axis=-1)
    a1, a2 = half(ang_first), half(ang_second)
    s = a1.shape[0]
    if a2 is None:
        cos = jnp.concatenate([jnp.cos(a1), jnp.ones((s, 64), F32)], -1)
        sin = jnp.concatenate([jnp.sin(a1), jnp.zeros((s, 64), F32)], -1)
    else:
        cos = jnp.concatenate([jnp.cos(a1), jnp.cos(a2)], -1)
        sin = jnp.concatenate([jnp.sin(a1), jnp.sin(a2)], -1)
    first = (jnp.arange(LANE) % 64) < 32
    return cos, jnp.where(first, -sin, 0.0), jnp.where(first, 0.0, sin)


def _tables(seq):
    rows = seq // GRID_W
    row = jnp.repeat(jnp.arange(rows), GRID_W).astype(F32)
    col = jnp.tile(jnp.arange(GRID_W), rows).astype(F32)
    t = jnp.arange(seq).astype(F32)
    inv_ax = ROPE_THETA ** (-jnp.arange(0, HEAD_DIM // 2, 2, dtype=F32) / (HEAD_DIM // 2))
    inv_t = ROPE_THETA ** (-jnp.arange(0, C_ROPE, 2, dtype=F32) / C_ROPE)
    ax = _rope_tabs(row[:, None] * inv_ax[None], col[:, None] * inv_ax[None])
    tt = _rope_tabs(t[:, None] * inv_t[None], None)
    return ax, tt


def _prep_weights(p):
    off, total = _layout()
    d = D_MODEL
    L = p['w_in'].shape[0]
    splits = np.cumsum([A_HEADS * HEAD_DIM, A_KV_HEADS * HEAD_DIM, A_KV_HEADS * HEAD_DIM, B_HEADS * HEAD_DIM,
                        B_KV_HEADS * HEAD_DIM, B_KV_HEADS * HEAD_DIM, C_Q_RANK, C_KV_RANK, C_ROPE])
    qa, ka, va, qb, kb, vb, cq, ckv, kr, g = jnp.split(p['w_in'].astype(BF16), [int(s) for s in splits], axis=-1)
    z = lambda w: jnp.zeros((L, d, w), BF16)
    krp = jnp.concatenate([z(LANE), kr, z(C_PAD - LANE - C_ROPE)], -1)
    cols = dict(g=g, qa=qa, qb=qb, cq=cq, ckv=ckv, ka=ka, va=va, kb=kb, vb=vb, kr=krp)
    parts, o = [], 0
    for name, (start, width) in off.items():
        assert start == o and cols[name].shape[-1] == width
        parts.append(cols[name])
        o += width
    parts.append(z(total - o))
    w_in = jnp.concatenate(parts, -1)

    wuq = p['w_uq'].astype(BF16).reshape(L, C_Q_RANK, C_HEADS, C_NOPE + C_ROPE)
    wuq = jnp.pad(wuq, ((0, 0), (0, 0), (0, 0), (0, C_PAD - C_NOPE - C_ROPE))).reshape(L, C_Q_RANK, C_HEADS * C_PAD)
    wukv = p['w_ukv'].astype(BF16).reshape(L, C_KV_RANK, C_HEADS, C_NOPE + C_V)
    wk = wukv[..., :C_NOPE].reshape(L, C_KV_RANK, C_HEADS * C_NOPE)
    wv = wukv[..., C_NOPE:].reshape(L, C_KV_RANK, C_HEADS * C_V)
    wr = jnp.pad(p['w_router'], ((0, 0), (0, 0), (0, LANE - N_EXPERTS))).astype(BF16)
    row = lambda a: a[:, None, :]
    g_heads = A_HEADS // A_KV_HEADS
    sink = jnp.broadcast_to(p['a_sink'].reshape(L, A_KV_HEADS, g_heads, 1, 1),
                            (L, A_KV_HEADS, g_heads, BLOCK, 1)).reshape(L, A_KV_HEADS, g_heads * BLOCK, 1)
    return dict(w_in=w_in, b_gate=row(p['b_gate']), gqb=row(p['b_q_norm']), gkb=row(p['b_k_norm']),
                gcq=row(p['c_q_norm']), gckv=row(p['c_kv_norm']), wuq=wuq, wk=wk, wv=wv, sink=sink,
                wa=p['w_branch_a'].astype(BF16), wb=p['w_branch_b'].astype(BF16), wc=p['w_branch_c'].astype(BF16),
                wo=p['w_o'].astype(BF16), ln1_g=row(p['ln1_g']), ln1_b=row(p['ln1_b']), wr=wr,
                weg=p['w_e_gate'].astype(BF16), weu=p['w_e_up'].astype(BF16), wed=p['w_e_down'].astype(BF16),
                ln2_g=row(p['ln2_g']), ln2_b=row(p['ln2_b']))


def _layer(x, xb, w, bias, tabs, bsz, seq):
    off, _ = _layout()
    n = bsz * seq
    ax_tabs, t_tabs = tabs
    h = _matmul(xb, w['w_in'], BF16, 2048, 512)
    oa = _window_attention(h, bias, w['sink'], bsz, seq)
    qb, kb = _b_prep(h, ax_tabs, w['gqb'], w['gkb'], seq, 512)
    vb = h[:, off['vb'][0]:off['vb'][0] + off['vb'][1]]
    ob = _flash(qb, kb, _kv_major_t(vb, bsz, seq, B_KV_HEADS, HEAD_DIM), bsz, seq, B_HEADS, B_KV_HEADS, HEAD_DIM,
                HEAD_DIM, 8192)
    qc, kc, vc = _c_prep(h, w['wuq'], w['wk'], w['wv'], w['gcq'], w['gckv'], t_tabs, seq, 512)
    oc = _flash(qc, kc, _kv_major_t(vc, bsz, seq, C_HEADS, C_V), bsz, seq, C_HEADS, C_HEADS, C_PAD, C_V, 8192)
    merged = _merge(oa, ob, oc, w['wa'], w['wb'], w['wc'], h, w['b_gate'], 1024, 1024)
    x1, acc0, logits = _wo_ln(merged, w['wo'], x, w['ln1_g'], w['ln1_b'], w['wr'], 512)
    cap = EC_FACTOR * n // N_EXPERTS
    aff, thr, ngt = _route(logits[:, :N_EXPERTS].T, cap)
    idx, gval = _select_indices(aff, thr, ngt, cap)
    acc = _moe_ffn(x1, acc0, idx, gval, w['weg'], w['weu'], w['wed'], 1024, 512)
    return _ln_out(acc, w['ln2_g'], w['ln2_b'], 512)


def _rel_bias(rel_table):
    offs = jnp.arange(3 * BLOCK)[None, :] - BLOCK - jnp.arange(BLOCK)[:, None]
    rb = jnp.transpose(rel_table[_t5_bucket(offs)], (2, 0, 1)).astype(F32)
    g = A_HEADS // A_KV_HEADS
    return rb.reshape(A_KV_HEADS, g * BLOCK, 3 * BLOCK)


def kernel(x_prompt, x_sample, rel_table, w_in, b_gate, b_q_norm, b_k_norm, c_q_norm, c_kv_norm, w_uq, w_ukv, a_sink,
           w_branch_a, w_branch_b, w_branch_c, w_o, ln1_g, ln1_b, w_router, w_e_gate, w_e_up, w_e_down, ln2_g, ln2_b):
    params = dict(w_in=w_in, b_gate=b_gate, b_q_norm=b_q_norm, b_k_norm=b_k_norm, c_q_norm=c_q_norm,
                  c_kv_norm=c_kv_norm, w_uq=w_uq, w_ukv=w_ukv, a_sink=a_sink, w_branch_a=w_branch_a,
                  w_branch_b=w_branch_b, w_branch_c=w_branch_c, w_o=w_o, ln1_g=ln1_g, ln1_b=ln1_b,
                  w_router=w_router, w_e_gate=w_e_gate, w_e_up=w_e_up, w_e_down=w_e_down, ln2_g=ln2_g, ln2_b=ln2_b)
    w_all = _prep_weights(params)
    bias = _rel_bias(rel_table)
    groups = []
    for x in (x_prompt, x_sample):
        bsz, seq, d = x.shape
        groups.append((bsz, seq, _tables(seq)))
    xs = tuple(x.reshape(-1, x.shape[-1]) for x in (x_prompt, x_sample))
    carry = tuple((x, x.astype(BF16)) for x in xs)

    def body(carry, w):
        out = []
        for (x, xb), (bsz, seq, tabs) in zip(carry, groups):
            out.append(_layer(x, xb, w, bias, tabs, bsz, seq))
        return tuple(out), None

    carry, _ = lax.scan(body, carry, w_all)
    return tuple(c[0].reshape(xin.shape) for c, xin in zip(carry, (x_prompt, x_sample)))
```

```python
import functools
import math

import jax
import jax.numpy as jnp
import numpy as np
from jax import lax
from jax.experimental import pallas as pl
from jax.experimental.pallas import tpu as pltpu

D_MODEL = 2048
DEPTH = 4
GRID_W = 64
BLOCK = 128
HEAD_DIM = 128
A_HEADS = 8
A_KV_HEADS = 2
WINDOW = 128
B_HEADS = 8
B_KV_HEADS = 2
C_HEADS = 8
C_Q_RANK = 512
C_KV_RANK = 512
C_NOPE = 128
C_ROPE = 64
C_V = 128
ROPE_THETA = 10000.0
N_BUCKETS = 32
MAX_DIST = 128
N_EXPERTS = 16
EXPERT_FF = 2048
EC_FACTOR = 2
N_BRANCH = 3
EPS = 1e-6
NEG = -1e30
LOG2E = 1.4426950408889634

LANE = 128
C_PAD = 2 * LANE
VMEM_LIMIT = 56 * 1024 * 1024

BF16 = jnp.bfloat16
F32 = jnp.float32


def _dn_alpha():
    return (2 * DEPTH) ** 0.25


def _layout():
    d = D_MODEL
    segs = [('g', N_BRANCH * d), ('qa', A_HEADS * HEAD_DIM), ('qb', B_HEADS * HEAD_DIM), ('cq', C_Q_RANK),
            ('ckv', C_KV_RANK), ('ka', A_KV_HEADS * HEAD_DIM), ('va', A_KV_HEADS * HEAD_DIM),
            ('kb', B_KV_HEADS * HEAD_DIM), ('vb', B_KV_HEADS * HEAD_DIM), ('kr', C_PAD)]
    off, o = {}, 0
    for name, w in segs:
        off[name] = (o, w)
        o += w
    total = -(-o // 512) * 512
    return off, total


def _cparams(sem, vmem=VMEM_LIMIT):
    return pltpu.CompilerParams(dimension_semantics=sem, vmem_limit_bytes=vmem)


def _mm_kernel(x_ref, w_ref, o_ref):
    o_ref[...] = jnp.dot(x_ref[...], w_ref[...], preferred_element_type=F32).astype(o_ref.dtype)


def _matmul(x, w, out_dtype, tm, tn):
    m, k = x.shape
    n = w.shape[1]
    tm, tn = min(tm, m), min(tn, n)
    return pl.pallas_call(
        _mm_kernel,
        grid=(m // tm, n // tn),
        in_specs=[pl.BlockSpec((tm, k), lambda i, j: (i, 0)), pl.BlockSpec((k, tn), lambda i, j: (0, j))],
        out_specs=pl.BlockSpec((tm, tn), lambda i, j: (i, j)),
        out_shape=jax.ShapeDtypeStruct((m, n), out_dtype),
        compiler_params=_cparams(("parallel", "parallel")),
        name="in_proj",
    )(x, w)


def _rope(x, cos, sa, sb):
    return x * cos + pltpu.roll(x, LANE - 32, 1) * sa + pltpu.roll(x, 32, 1) * sb


def _rms(x, g):
    return x * lax.rsqrt(jnp.mean(x * x, axis=-1, keepdims=True) + EPS) * g


def _bprep_kernel(q_ref, k_ref, cos_ref, sa_ref, sb_ref, gq_ref, gk_ref, qo_ref, ko_ref, *, scale):
    cos, sa, sb = cos_ref[...], sa_ref[...], sb_ref[...]
    gq, gk = gq_ref[...], gk_ref[...]
    for h in range(B_HEADS):
        x = q_ref[:, h * LANE:(h + 1) * LANE].astype(F32)
        qo_ref[:, h * LANE:(h + 1) * LANE] = (_rope(_rms(x, gq), cos, sa, sb) * scale).astype(BF16)
    for h in range(B_KV_HEADS):
        x = k_ref[:, h * LANE:(h + 1) * LANE].astype(F32)
        ko_ref[:, h * LANE:(h + 1) * LANE] = _rope(_rms(x, gk), cos, sa, sb).astype(BF16)


def _b_prep(h, tabs, gq, gk, seq, tm):
    off, _ = _layout()
    n = h.shape[0]
    tm = min(tm, seq)
    spb = seq // tm
    qw, kw = B_HEADS * HEAD_DIM, B_KV_HEADS * HEAD_DIM
    qo, ko = off['qb'][0] // qw, off['kb'][0] // kw
    tab_spec = pl.BlockSpec((tm, LANE), lambda i: (i % spb, 0))
    vec_spec = pl.BlockSpec((1, LANE), lambda i: (0, 0))
    scale = HEAD_DIM ** -0.5 * LOG2E
    return pl.pallas_call(
        functools.partial(_bprep_kernel, scale=scale),
        grid=(n // tm,),
        in_specs=[pl.BlockSpec((tm, qw), lambda i: (i, qo)), pl.BlockSpec((tm, kw), lambda i: (i, ko)),
                  tab_spec, tab_spec, tab_spec, vec_spec, vec_spec],
        out_specs=[pl.BlockSpec((tm, qw), lambda i: (i, 0)), pl.BlockSpec((tm, kw), lambda i: (i, 0))],
        out_shape=[jax.ShapeDtypeStruct((n, qw), BF16), jax.ShapeDtypeStruct((n, kw), BF16)],
        compiler_params=_cparams(("parallel",)),
        name="b_prep",
    )(h, h, *tabs, gq, gk)


def _cprep_kernel(cq_ref, ckv_ref, kr_ref, wq_ref, wk_ref, wv_ref, gq_ref, gkv_ref, cos_ref, sa_ref, sb_ref,
                  qo_ref, ko_ref, vo_ref, *, scale):
    cos, sa, sb = cos_ref[...], sa_ref[...], sb_ref[...]
    cqn = _rms(cq_ref[...].astype(F32), gq_ref[...]).astype(BF16)
    ckvn = _rms(ckv_ref[...].astype(F32), gkv_ref[...]).astype(BF16)
    q = jnp.dot(cqn, wq_ref[...], preferred_element_type=F32)
    kn = jnp.dot(ckvn, wk_ref[...], preferred_element_type=F32)
    vo_ref[...] = jnp.dot(ckvn, wv_ref[...], preferred_element_type=F32).astype(BF16)
    kr = _rope(kr_ref[:, LANE:].astype(F32), cos, sa, sb).astype(BF16)
    for h in range(C_HEADS):
        b = h * C_PAD
        qo_ref[:, b:b + LANE] = (q[:, b:b + LANE] * scale).astype(BF16)
        qo_ref[:, b + LANE:b + C_PAD] = (_rope(q[:, b + LANE:b + C_PAD], cos, sa, sb) * scale).astype(BF16)
        ko_ref[:, b:b + LANE] = kn[:, h * LANE:(h + 1) * LANE].astype(BF16)
        ko_ref[:, b + LANE:b + C_PAD] = kr


def _c_prep(h, wq, wk, wv, gq, gkv, tabs, seq, tm):
    off, _ = _layout()
    n = h.shape[0]
    tm = min(tm, seq)
    spb = seq // tm
    cqo, ckvo, kro = off['cq'][0] // C_Q_RANK, off['ckv'][0] // C_KV_RANK, off['kr'][0] // C_PAD
    tab_spec = pl.BlockSpec((tm, LANE), lambda i: (i % spb, 0))
    full = lambda a: pl.BlockSpec(a.shape, lambda i: (0, 0))
    scale = (C_NOPE + C_ROPE) ** -0.5 * LOG2E
    qw, vw = C_HEADS * C_PAD, C_HEADS * C_V
    return pl.pallas_call(
        functools.partial(_cprep_kernel, scale=scale),
        grid=(n // tm,),
        in_specs=[pl.BlockSpec((tm, C_Q_RANK), lambda i: (i, cqo)), pl.BlockSpec((tm, C_KV_RANK), lambda i: (i, ckvo)),
                  pl.BlockSpec((tm, C_PAD), lambda i: (i, kro)), full(wq), full(wk), full(wv), full(gq), full(gkv),
                  tab_spec, tab_spec, tab_spec],
        out_specs=[pl.BlockSpec((tm, qw), lambda i: (i, 0)), pl.BlockSpec((tm, qw), lambda i: (i, 0)),
                   pl.BlockSpec((tm, vw), lambda i: (i, 0))],
        out_shape=[jax.ShapeDtypeStruct((n, qw), BF16), jax.ShapeDtypeStruct((n, qw), BF16),
                   jax.ShapeDtypeStruct((n, vw), BF16)],
        compiler_params=_cparams(("parallel",)),
        name="c_prep",
    )(h, h, h, wq, wk, wv, gq, gkv, *tabs)


WIN_QB = 2
ONES_ROWS = 16


def _reduce_rows(x, op):
    acc = x[:8]
    for r in range(8, x.shape[0], 8):
        acc = op(acc, x[r:r + 8])
    while acc.shape[0] > 1:
        half = acc.shape[0] // 2
        acc = op(acc[:half], acc[half:])
    return acc[0]


def _win_kernel(q_ref, kp_ref, kc_ref, kn_ref, vtp_ref, vtc_ref, vtn_ref, bias_ref, sink_ref, o_ref, *, nb, scale):
    step = pl.program_id(1)
    g = A_HEADS // A_KV_HEADS
    ones = jnp.ones((ONES_ROWS, 3 * BLOCK), BF16)
    for u in range(WIN_QB):
        blk = step * WIN_QB + u
        edge = jnp.where(blk == 0, 1, jnp.where(blk == nb - 1, 2, 0))
        rs = slice(u * BLOCK, (u + 1) * BLOCK)
        for hk in range(A_KV_HEADS):
            ks = slice(hk * HEAD_DIM, (hk + 1) * HEAD_DIM)
            kparts = [kp_ref[:, ks]] + [kc_ref[w * BLOCK:(w + 1) * BLOCK, ks] for w in range(WIN_QB)] + [kn_ref[:, ks]]
            vtparts = [vtp_ref[0, hk, 0]] + [vtc_ref[0, hk, w] for w in range(WIN_QB)] + [vtn_ref[0, hk, 0]]
            k = jnp.concatenate(kparts[u:u + 3], axis=0)
            vt = jnp.concatenate([jnp.concatenate(vtparts[u:u + 3], axis=1), ones], axis=0)
            q = jnp.concatenate([q_ref[rs, (hk * g + j) * HEAD_DIM:(hk * g + j + 1) * HEAD_DIM] for j in range(g)],
                                axis=0)
            st = (lax.dot_general(k, q, (((1,), (1,)), ((), ())), preferred_element_type=F32) * scale
                  + bias_ref[edge, hk])
            sink = sink_ref[hk]
            colmax = jnp.max(_reduce_rows(st.reshape(3 * BLOCK // 8, 8, g * BLOCK), jnp.maximum), axis=0, keepdims=True)
            m = jnp.maximum(colmax, sink)
            p = jnp.exp(st - m).astype(BF16)
            acc = jnp.dot(vt, p, preferred_element_type=F32)
            denom = acc[HEAD_DIM:HEAD_DIM + 1] + jnp.exp(sink - m)
            o = (acc[:HEAD_DIM] / denom).T.astype(BF16)
            for j in range(g):
                o_ref[rs, (hk * g + j) * HEAD_DIM:(hk * g + j + 1) * HEAD_DIM] = o[j * BLOCK:(j + 1) * BLOCK]


def _window_attention(h, vta, bias, sink, bsz, seq):
    off, _ = _layout()
    nb = seq // BLOCK
    assert nb % WIN_QB == 0 and nb >= 2
    ns = nb // WIN_QB
    qw, kw = A_HEADS * HEAD_DIM, A_KV_HEADS * HEAD_DIM
    qo, ko = off['qa'][0] // qw, off['ka'][0] // kw
    cur = lambda b, n: b * ns + n
    prv = lambda n: jnp.maximum(n * WIN_QB - 1, 0)
    nxt = lambda n: jnp.minimum((n + 1) * WIN_QB, nb - 1)
    kedge = lambda f: pl.BlockSpec((BLOCK, kw), lambda b, n: (b * nb + f(n), ko))
    vedge = lambda f: pl.BlockSpec((1, A_KV_HEADS, 1, HEAD_DIM, BLOCK), lambda b, n: (b, 0, f(n), 0, 0))
    full = lambda a: pl.BlockSpec(a.shape, lambda b, n: (0,) * a.ndim)
    return pl.pallas_call(
        functools.partial(_win_kernel, nb=nb, scale=HEAD_DIM ** -0.5),
        grid=(bsz, ns),
        in_specs=[pl.BlockSpec((WIN_QB * BLOCK, qw), lambda b, n: (cur(b, n), qo)),
                  kedge(prv), pl.BlockSpec((WIN_QB * BLOCK, kw), lambda b, n: (cur(b, n), ko)), kedge(nxt),
                  vedge(prv), pl.BlockSpec((1, A_KV_HEADS, WIN_QB, HEAD_DIM, BLOCK), lambda b, n: (b, 0, n, 0, 0)),
                  vedge(nxt), full(bias), full(sink)],
        out_specs=pl.BlockSpec((WIN_QB * BLOCK, qw), lambda b, n: (cur(b, n), 0)),
        out_shape=jax.ShapeDtypeStruct((bsz * seq, qw), BF16),
        compiler_params=_cparams(("parallel", "parallel")),
        name="window_attn",
    )(h, h, h, h, vta, vta, vta, bias, sink)


Q_CHUNK = 2 * LANE
KV_SUB = 512


def _flash_kernel(q_ref, k_ref, vt_ref, o_ref, m_sc, acc_sc, s0_sc, s1_sc, mx0_sc, mx1_sc, *,
                  nsub, nchunk, g, dq, dv):
    ki = pl.program_id(3)

    @pl.when(ki == 0)
    def _():
        m_sc[...] = jnp.full(m_sc.shape, -jnp.inf, F32)
        acc_sc[...] = jnp.zeros(acc_sc.shape, F32)

    def q_chunk(j):
        return q_ref[:, j * dq:(j + 1) * dq] if g > 1 else q_ref[j * Q_CHUNK:(j + 1) * Q_CHUNK, :]

    def scores(t, s_ref, mx_ref):
        k = k_ref[pl.ds(pl.multiple_of(t * KV_SUB, KV_SUB), KV_SUB), :]
        for j in range(nchunk):
            cs = slice(j * Q_CHUNK, (j + 1) * Q_CHUNK)
            st = lax.dot_general(k, q_chunk(j), (((1,), (1,)), ((), ())), preferred_element_type=F32)
            s_ref[:, cs] = st
            mx_ref[:, cs] = _reduce_rows(st.reshape(KV_SUB // 8, 8, Q_CHUNK), jnp.maximum)

    def softmax_pv(t, s_ref, mx_ref):
        vt = jnp.concatenate([vt_ref[0, 0, t], jnp.ones((ONES_ROWS, KV_SUB), BF16)], axis=0)
        for j in range(nchunk):
            cs = slice(j * Q_CHUNK, (j + 1) * Q_CHUNK)
            m_prev = m_sc[:, cs]
            m_new = jnp.maximum(m_prev, jnp.max(mx_ref[:, cs], axis=0, keepdims=True))
            alpha = jnp.exp2(m_prev - m_new)
            p = jnp.exp2(s_ref[:, cs] - m_new).astype(BF16)
            acc_sc[:, cs] = alpha * acc_sc[:, cs] + jnp.dot(vt, p, preferred_element_type=F32)
            m_sc[:, cs] = m_new

    scores(0, s0_sc, mx0_sc)

    def body(i, carry):
        t = 2 * i
        scores(t + 1, s1_sc, mx1_sc)
        softmax_pv(t, s0_sc, mx0_sc)
        scores(t + 2, s0_sc, mx0_sc)
        softmax_pv(t + 1, s1_sc, mx1_sc)
        return carry

    lax.fori_loop(0, nsub // 2 - 1, body, 0)
    scores(nsub - 1, s1_sc, mx1_sc)
    softmax_pv(nsub - 2, s0_sc, mx0_sc)
    softmax_pv(nsub - 1, s1_sc, mx1_sc)

    @pl.when(ki == pl.num_programs(3) - 1)
    def _():
        for j in range(nchunk):
            cs = slice(j * Q_CHUNK, (j + 1) * Q_CHUNK)
            o = (acc_sc[:dv, cs] / acc_sc[dv:dv + 1, cs]).T.astype(o_ref.dtype)
            if g > 1:
                o_ref[:, j * dv:(j + 1) * dv] = o
            else:
                o_ref[j * Q_CHUNK:(j + 1) * Q_CHUNK, :] = o


def _kv_major_t(v, bsz, seq, hk, dv):
    sub = min(KV_SUB, seq)
    return v.reshape(bsz, seq // sub, sub, hk, dv).transpose(0, 3, 1, 4, 2)


def _flash(q, k, vt, bsz, seq, hq, hk, dq, dv, bk):
    g = hq // hk
    assert seq % KV_SUB == 0 and (g == 1 or g * Q_CHUNK <= 4 * Q_CHUNK)
    bk = min(bk, seq)
    nchunk = g if g > 1 else 4
    bq = Q_CHUNK if g > 1 else nchunk * Q_CHUNK
    assert seq % bq == 0 and seq % bk == 0 and bk % (2 * KV_SUB) == 0
    nq, nk, nsub = seq // bq, seq // bk, bk // KV_SUB
    return pl.pallas_call(
        functools.partial(_flash_kernel, nsub=nsub, nchunk=nchunk, g=g, dq=dq, dv=dv),
        grid=(bsz, hk, nq, nk),
        in_specs=[pl.BlockSpec((bq, g * dq), lambda b, h, i, j: (b * nq + i, h)),
                  pl.BlockSpec((bk, dq), lambda b, h, i, j: (b * nk + j, h)),
                  pl.BlockSpec((1, 1, nsub, dv, KV_SUB), lambda b, h, i, j: (b, h, j, 0, 0))],
        out_specs=pl.BlockSpec((bq, g * dv), lambda b, h, i, j: (b * nq + i, h)),
        out_shape=jax.ShapeDtypeStruct((bsz * seq, hq * dv), BF16),
        scratch_shapes=[pltpu.VMEM((1, nchunk * Q_CHUNK), F32),
                        pltpu.VMEM((dv + ONES_ROWS, nchunk * Q_CHUNK), F32), pltpu.VMEM((KV_SUB, nchunk * Q_CHUNK), F32),
                        pltpu.VMEM((KV_SUB, nchunk * Q_CHUNK), F32), pltpu.VMEM((8, nchunk * Q_CHUNK), F32),
                        pltpu.VMEM((8, nchunk * Q_CHUNK), F32)],
        compiler_params=_cparams(("parallel", "parallel", "parallel", "arbitrary")),
        name="dense_attn",
    )(q, k, vt)


def _merge_kernel(oa_ref, ob_ref, oc_ref, wa_ref, wb_ref, wc_ref, ga_ref, gb_ref, gc_ref, ba_ref, bb_ref, bc_ref,
                  o_ref):
    def branch(o_r, w_r, g_r, b_r):
        gate = jax.nn.sigmoid(g_r[...].astype(F32) + b_r[...])
        return gate * jnp.dot(o_r[...], w_r[...], preferred_element_type=F32)

    o_ref[...] = (branch(oa_ref, wa_ref, ga_ref, ba_ref) + branch(ob_ref, wb_ref, gb_ref, bb_ref)
                  + branch(oc_ref, wc_ref, gc_ref, bc_ref)).astype(o_ref.dtype)


def _merge(oa, ob, oc, wa, wb, wc, h, b_gate, tm, tn):
    n, kdim = oa.shape
    d = D_MODEL
    tm, tn = min(tm, n), min(tn, d)
    nj = d // tn
    act = pl.BlockSpec((tm, kdim), lambda i, j: (i, 0))
    wsp = pl.BlockSpec((kdim, tn), lambda i, j: (0, j))
    gate = lambda br: pl.BlockSpec((tm, tn), lambda i, j: (i, br * nj + j))
    bias = lambda br: pl.BlockSpec((1, tn), lambda i, j: (0, br * nj + j))
    return pl.pallas_call(
        _merge_kernel,
        grid=(n // tm, nj),
        in_specs=[act, act, act, wsp, wsp, wsp, gate(0), gate(1), gate(2), bias(0), bias(1), bias(2)],
        out_specs=pl.BlockSpec((tm, tn), lambda i, j: (i, j)),
        out_shape=jax.ShapeDtypeStruct((n, d), BF16),
        compiler_params=_cparams(("parallel", "parallel")),
        name="branch_merge",
    )(oa, ob, oc, wa, wb, wc, h, h, h, b_gate, b_gate, b_gate)


def _layer_norm(z, g, b):
    zc = z - jnp.mean(z, axis=-1, keepdims=True)
    var = jnp.mean(zc * zc, axis=-1, keepdims=True)
    return zc * lax.rsqrt(var + EPS) * g + b


def _woln_kernel(m_ref, wo_ref, x_ref, g_ref, b_ref, wr_ref, x1_ref, acc_ref, lg_ref, *, alpha):
    y = jnp.dot(m_ref[...], wo_ref[...], preferred_element_type=F32)
    x1 = _layer_norm(alpha * x_ref[...] + y, g_ref[...], b_ref[...])
    x1_ref[...] = x1
    acc_ref[...] = alpha * x1
    lg_ref[...] = jnp.dot(x1.astype(BF16), wr_ref[...], preferred_element_type=F32)


def _wo_ln(merged, wo, x, g, b, wr, tm):
    n, d = x.shape
    tm = min(tm, n)
    row = lambda w: pl.BlockSpec((tm, w), lambda i: (i, 0))
    full = lambda a: pl.BlockSpec(a.shape, lambda i: (0, 0))
    return pl.pallas_call(
        functools.partial(_woln_kernel, alpha=_dn_alpha()),
        grid=(n // tm,),
        in_specs=[row(d), full(wo), row(d), full(g), full(b), full(wr)],
        out_specs=[row(d), row(d), row(LANE)],
        out_shape=[jax.ShapeDtypeStruct((n, d), F32), jax.ShapeDtypeStruct((n, d), F32),
                   jax.ShapeDtypeStruct((n, LANE), F32)],
        compiler_params=_cparams(("parallel",)),
        name="wo_ln_router",
    )(merged, wo, x, g, b, wr)


def _route_kernel(lg_ref, aff_ref, thr_ref, ngt_ref, *, cap):
    lg = lg_ref[...]
    e = jnp.exp(lg - jnp.max(lg, axis=0, keepdims=True))
    aff = e / jnp.sum(e, axis=0, keepdims=True)
    aff_ref[...] = aff
    bits = lax.bitcast_convert_type(aff, jnp.int32)

    def step(i, t):
        cand = t | (jnp.int32(1) << (30 - i))
        cnt = jnp.sum((bits >= cand).astype(jnp.int32), axis=1, keepdims=True)
        return jnp.where(cnt >= cap, cand, t)

    thr = lax.fori_loop(0, 31, step, jnp.zeros((lg.shape[0], 1), jnp.int32))
    ngt = jnp.sum((bits > thr).astype(jnp.int32), axis=1, keepdims=True)
    thr_ref[...] = jnp.broadcast_to(thr, thr_ref.shape)
    ngt_ref[...] = jnp.broadcast_to(ngt, ngt_ref.shape)


def _route(logits_t, cap):
    e, n = logits_t.shape
    full = lambda s: pl.BlockSpec(s, lambda: (0, 0))
    return pl.pallas_call(
        functools.partial(_route_kernel, cap=cap),
        in_specs=[full((e, n))],
        out_specs=[full((e, n)), full((e, LANE)), full((e, LANE))],
        out_shape=[jax.ShapeDtypeStruct((e, n), F32), jax.ShapeDtypeStruct((e, LANE), jnp.int32),
                   jax.ShapeDtypeStruct((e, LANE), jnp.int32)],
        compiler_params=_cparams(None),
        name="route_select",
    )(logits_t)


def _select_indices(aff, thr, ngt, cap):
    e, n = aff.shape
    bits = lax.bitcast_convert_type(aff, jnp.int32)
    t, g = thr[:, :1], ngt[:, :1]
    gt, eq = bits > t, bits == t
    sel = gt | (eq & (jnp.cumsum(eq.astype(jnp.int32), axis=1) <= cap - g))
    tok = jnp.arange(n, dtype=jnp.int32)[None]
    idx, gval = lax.sort((jnp.where(sel, tok, tok + n), aff), dimension=1, num_keys=1)
    return idx[:, :cap], gval[:, :cap]


MOE_DMA_UNROLL = 8


def _moe_kernel(idx_ref, nxt_ref, x_hbm, acc_in, gv_ref, wg_ref, wu_ref, wd_ref, acc_out, xbuf, obuf, xb, hacc,
                sem_x, sem_o, sem_w, *, tm):
    e, i, f = pl.program_id(0), pl.program_id(1), pl.program_id(2)
    nf = pl.num_programs(2)
    tile = e * pl.num_programs(1) + i
    last_tile = pl.num_programs(0) * pl.num_programs(1) - 1

    def rows(ids_ref, make_copy):
        def issue(r0, c):
            for u in range(MOE_DMA_UNROLL):
                r = r0 * MOE_DMA_UNROLL + u
                make_copy(ids_ref[0, 0, r], r).start()
            return c

        lax.fori_loop(0, tm // MOE_DMA_UNROLL, issue, 0)

    x_row = lambda t, r: pltpu.make_async_copy(x_hbm.at[pl.ds(t, 1)], xbuf.at[pl.ds(r, 1)], sem_x)
    acc_row = lambda t, r: pltpu.make_async_copy(acc_in.at[pl.ds(t, 1)], obuf.at[pl.ds(r, 1)], sem_o)
    out_row = lambda t, r: pltpu.make_async_copy(obuf.at[pl.ds(r, 1)], acc_out.at[pl.ds(t, 1)], sem_w)
    wait_x = lambda: pltpu.make_async_copy(x_hbm.at[pl.ds(0, tm)], xbuf, sem_x).wait()
    wait_acc = lambda: pltpu.make_async_copy(acc_in.at[pl.ds(0, tm)], obuf, sem_o).wait()
    wait_out = lambda: pltpu.make_async_copy(obuf, acc_out.at[pl.ds(0, tm)], sem_w).wait()

    @pl.when((f == 0) & (tile == 0))
    def _():
        rows(idx_ref, x_row)

    @pl.when(f == 0)
    def _():
        wait_x()
        xb[...] = xbuf[...].astype(BF16)
        hacc[...] = jnp.zeros(hacc.shape, F32)

    @pl.when((f == 1) & (tile > 0))
    def _():
        wait_out()

    @pl.when(f == 1)
    def _():
        rows(idx_ref, acc_row)

    @pl.when((f == 2) & (tile < last_tile))
    def _():
        rows(nxt_ref, x_row)

    x = xb[...]
    gate = jnp.dot(x, wg_ref[0], preferred_element_type=F32)
    up = jnp.dot(x, wu_ref[0], preferred_element_type=F32)
    hh = (gate * jax.nn.sigmoid(gate) * up).astype(BF16)
    hacc[...] += jnp.dot(hh, wd_ref[0], preferred_element_type=F32)

    @pl.when(f == nf - 1)
    def _():
        wait_acc()
        obuf[...] = obuf[...] + hacc[...] * gv_ref[0]
        rows(idx_ref, out_row)

    @pl.when((f == nf - 1) & (tile == last_tile))
    def _():
        wait_out()


def _moe_ffn(x1, acc0, idx, gval, wg, wu, wd, tm, tf):
    n, d = x1.shape
    e, cap = idx.shape
    ff = wg.shape[2]
    tm, tf = min(tm, cap), min(tf, ff)
    nt, nf = cap // tm, ff // tf
    assert nf >= 4 and tm % MOE_DMA_UNROLL == 0
    idx3 = idx.reshape(e * nt, 1, tm)
    gv3 = gval.reshape(e * nt, tm, 1)
    anyspec = pl.BlockSpec(memory_space=pl.ANY)
    last = e * nt - 1
    return pl.pallas_call(
        functools.partial(_moe_kernel, tm=tm),
        grid=(e, nt, nf),
        in_specs=[pl.BlockSpec((1, 1, tm), lambda ei, i, f: (ei * nt + i, 0, 0), memory_space=pltpu.SMEM),
                  pl.BlockSpec((1, 1, tm), lambda ei, i, f: (jnp.minimum(ei * nt + i + 1, last), 0, 0),
                               memory_space=pltpu.SMEM),
                  anyspec, anyspec,
                  pl.BlockSpec((1, tm, 1), lambda ei, i, f: (ei * nt + i, 0, 0)),
                  pl.BlockSpec((1, d, tf), lambda ei, i, f: (ei, 0, f)),
                  pl.BlockSpec((1, d, tf), lambda ei, i, f: (ei, 0, f)),
                  pl.BlockSpec((1, tf, d), lambda ei, i, f: (ei, f, 0))],
        out_specs=anyspec,
        out_shape=jax.ShapeDtypeStruct((n, d), F32),
        scratch_shapes=[pltpu.VMEM((tm, d), F32), pltpu.VMEM((tm, d), F32), pltpu.VMEM((tm, d), BF16),
                        pltpu.VMEM((tm, d), F32), pltpu.SemaphoreType.DMA(()), pltpu.SemaphoreType.DMA(()),
                        pltpu.SemaphoreType.DMA(())],
        input_output_aliases={3: 0},
        compiler_params=_cparams(("arbitrary", "arbitrary", "arbitrary")),
        name="moe_ffn",
    )(idx3, idx3, x1, acc0, gv3, wg, wu, wd)


def _ln_kernel(z_ref, g_ref, b_ref, x_ref, xb_ref):
    x = _layer_norm(z_ref[...], g_ref[...], b_ref[...])
    x_ref[...] = x
    xb_ref[...] = x.astype(BF16)


def _ln_out(z, g, b, tm):
    n, d = z.shape
    tm = min(tm, n)
    row = pl.BlockSpec((tm, d), lambda i: (i, 0))
    vec = pl.BlockSpec((1, d), lambda i: (0, 0))
    return pl.pallas_call(
        _ln_kernel,
        grid=(n // tm,),
        in_specs=[row, vec, vec],
        out_specs=[row, row],
        out_shape=[jax.ShapeDtypeStruct((n, d), F32), jax.ShapeDtypeStruct((n, d), BF16)],
        compiler_params=_cparams(("parallel",)),
        name="ln_out",
    )(z, g, b)


def _t5_bucket(rel):
    nb = N_BUCKETS // 2
    max_exact = nb // 2
    n = jnp.abs(rel)
    scaled = jnp.log(jnp.maximum(n, 1).astype(F32) / max_exact) / math.log(MAX_DIST / max_exact)
    large = jnp.minimum(max_exact + (scaled * (nb - max_exact)).astype(jnp.int32), nb - 1)
    return jnp.where(rel > 0, nb, 0) + jnp.where(n < max_exact, n, large)


def _rope_tabs(ang_first, ang_second):
    def half(ang):
        if ang is None:
            return None
        return jnp.concatenate([ang, ang], axis=-1)
    a1, a2 = half(ang_first), half(ang_second)
    s = a1.shape[0]
    if a2 is None:
        cos = jnp.concatenate([jnp.cos(a1), jnp.ones((s, 64), F32)], -1)
        sin = jnp.concatenate([jnp.sin(a1), jnp.zeros((s, 64), F32)], -1)
    else:
        cos = jnp.concatenate([jnp.cos(a1), jnp.cos(a2)], -1)
        sin = jnp.concatenate([jnp.sin(a1), jnp.sin(a2)], -1)
    first = (jnp.arange(LANE) % 64) < 32
    return cos, jnp.where(first, -sin, 0.0), jnp.where(first, 0.0, sin)


def _tables(seq):
    rows = seq // GRID_W
    row = jnp.repeat(jnp.arange(rows), GRID_W).astype(F32)
    col = jnp.tile(jnp.arange(GRID_W), rows).astype(F32)
    t = jnp.arange(seq).astype(F32)
    inv_ax = ROPE_THETA ** (-jnp.arange(0, HEAD_DIM // 2, 2, dtype=F32) / (HEAD_DIM // 2))
    inv_t = ROPE_THETA ** (-jnp.arange(0, C_ROPE, 2, dtype=F32) / C_ROPE)
    ax = _rope_tabs(row[:, None] * inv_ax[None], col[:, None] * inv_ax[None])
    tt = _rope_tabs(t[:, None] * inv_t[None], None)
    return ax, tt


def _prep_weights(p):
    off, total = _layout()
    d = D_MODEL
    L = p['w_in'].shape[0]
    splits = np.cumsum([A_HEADS * HEAD_DIM, A_KV_HEADS * HEAD_DIM, A_KV_HEADS * HEAD_DIM, B_HEADS * HEAD_DIM,
                        B_KV_HEADS * HEAD_DIM, B_KV_HEADS * HEAD_DIM, C_Q_RANK, C_KV_RANK, C_ROPE])
    qa, ka, va, qb, kb, vb, cq, ckv, kr, g = jnp.split(p['w_in'].astype(BF16), [int(s) for s in splits], axis=-1)
    z = lambda w: jnp.zeros((L, d, w), BF16)
    krp = jnp.concatenate([z(LANE), kr, z(C_PAD - LANE - C_ROPE)], -1)
    cols = dict(g=g, qa=qa, qb=qb, cq=cq, ckv=ckv, ka=ka, va=va, kb=kb, vb=vb, kr=krp)
    parts, o = [], 0
    for name, (start, width) in off.items():
        assert start == o and cols[name].shape[-1] == width
        parts.append(cols[name])
        o += width
    parts.append(z(total - o))
    w_in = jnp.concatenate(parts, -1)

    wuq = p['w_uq'].astype(BF16).reshape(L, C_Q_RANK, C_HEADS, C_NOPE + C_ROPE)
    wuq = jnp.pad(wuq, ((0, 0), (0, 0), (0, 0), (0, C_PAD - C_NOPE - C_ROPE))).reshape(L, C_Q_RANK, C_HEADS * C_PAD)
    wukv = p['w_ukv'].astype(BF16).reshape(L, C_KV_RANK, C_HEADS, C_NOPE + C_V)
    wk = wukv[..., :C_NOPE].reshape(L, C_KV_RANK, C_HEADS * C_NOPE)
    wv = wukv[..., C_NOPE:].reshape(L, C_KV_RANK, C_HEADS * C_V)
    wr = jnp.pad(p['w_router'], ((0, 0), (0, 0), (0, LANE - N_EXPERTS))).astype(BF16)
    row = lambda a: a[:, None, :]
    g_heads = A_HEADS // A_KV_HEADS
    sink = jnp.broadcast_to(p['a_sink'].reshape(L, A_KV_HEADS, 1, g_heads, 1),
                            (L, A_KV_HEADS, 1, g_heads, BLOCK)).reshape(L, A_KV_HEADS, 1, g_heads * BLOCK)
    return dict(w_in=w_in, b_gate=row(p['b_gate']), gqb=row(p['b_q_norm']), gkb=row(p['b_k_norm']),
                gcq=row(p['c_q_norm']), gckv=row(p['c_kv_norm']), wuq=wuq, wk=wk, wv=wv, sink=sink,
                wa=p['w_branch_a'].astype(BF16), wb=p['w_branch_b'].astype(BF16), wc=p['w_branch_c'].astype(BF16),
                wo=p['w_o'].astype(BF16), ln1_g=row(p['ln1_g']), ln1_b=row(p['ln1_b']), wr=wr,
                weg=p['w_e_gate'].astype(BF16), weu=p['w_e_up'].astype(BF16), wed=p['w_e_down'].astype(BF16),
                ln2_g=row(p['ln2_g']), ln2_b=row(p['ln2_b']))


def _layer(x, xb, w, bias, tabs, bsz, seq):
    off, _ = _layout()
    n = bsz * seq
    ax_tabs, t_tabs = tabs
    h = _matmul(xb, w['w_in'], BF16, 2048, 512)
    va = h[:, off['va'][0]:off['va'][0] + off['va'][1]]
    vta = va.reshape(bsz, seq // BLOCK, BLOCK, A_KV_HEADS, HEAD_DIM).transpose(0, 3, 1, 4, 2)
    oa = _window_attention(h, vta, bias, w['sink'], bsz, seq)
    qb, kb = _b_prep(h, ax_tabs, w['gqb'], w['gkb'], seq, 512)
    vb = h[:, off['vb'][0]:off['vb'][0] + off['vb'][1]]
    ob = _flash(qb, kb, _kv_major_t(vb, bsz, seq, B_KV_HEADS, HEAD_DIM), bsz, seq, B_HEADS, B_KV_HEADS, HEAD_DIM,
                HEAD_DIM, 8192)
    qc, kc, vc = _c_prep(h, w['wuq'], w['wk'], w['wv'], w['gcq'], w['gckv'], t_tabs, seq, 512)
    oc = _flash(qc, kc, _kv_major_t(vc, bsz, seq, C_HEADS, C_V), bsz, seq, C_HEADS, C_HEADS, C_PAD, C_V, 8192)
    merged = _merge(oa, ob, oc, w['wa'], w['wb'], w['wc'], h, w['b_gate'], 1024, 1024)
    x1, acc0, logits = _wo_ln(merged, w['wo'], x, w['ln1_g'], w['ln1_b'], w['wr'], 512)
    cap = EC_FACTOR * n // N_EXPERTS
    aff, thr, ngt = _route(logits[:, :N_EXPERTS].T, cap)
    idx, gval = _select_indices(aff, thr, ngt, cap)
    acc = _moe_ffn(x1, acc0, idx, gval, w['weg'], w['weu'], w['wed'], 1024, 512)
    return _ln_out(acc, w['ln2_g'], w['ln2_b'], 512)


def _rel_bias(rel_table):
    offs = jnp.arange(3 * BLOCK)[None, :] - BLOCK - jnp.arange(BLOCK)[:, None]
    rb = jnp.transpose(rel_table[_t5_bucket(offs)], (2, 0, 1)).astype(F32)
    rb = jnp.where(jnp.abs(offs)[None] <= WINDOW, rb, NEG)
    col = jnp.arange(3 * BLOCK)
    variants = jnp.stack([rb, jnp.where(col >= BLOCK, rb, NEG), jnp.where(col < 2 * BLOCK, rb, NEG)])
    g = A_HEADS // A_KV_HEADS
    return variants.reshape(3, A_KV_HEADS, g * BLOCK, 3 * BLOCK).transpose(0, 1, 3, 2)


def kernel(x_prompt, x_sample, rel_table, w_in, b_gate, b_q_norm, b_k_norm, c_q_norm, c_kv_norm, w_uq, w_ukv, a_sink,
           w_branch_a, w_branch_b, w_branch_c, w_o, ln1_g, ln1_b, w_router, w_e_gate, w_e_up, w_e_down, ln2_g, ln2_b):
    params = dict(w_in=w_in, b_gate=b_gate, b_q_norm=b_q_norm, b_k_norm=b_k_norm, c_q_norm=c_q_norm,
                  c_kv_norm=c_kv_norm, w_uq=w_uq, w_ukv=w_ukv, a_sink=a_sink, w_branch_a=w_branch_a,
                  w_branch_b=w_branch_b, w_branch_c=w_branch_c, w_o=w_o, ln1_g=ln1_g, ln1_b=ln1_b,
                  w_router=w_router, w_e_gate=w_e_gate, w_e_up=w_e_up, w_e_down=w_e_down, ln2_g=ln2_g, ln2_b=ln2_b)
    w_all = _prep_weights(params)
    bias = _rel_bias(rel_table)
    groups = []
    for x in (x_prompt, x_sample):
        bsz, seq, d = x.shape
        groups.append((bsz, seq, _tables(seq)))
    xs = tuple(x.reshape(-1, x.shape[-1]) for x in (x_prompt, x_sample))
    carry = tuple((x, x.astype(BF16)) for x in xs)

    def body(carry, w):
        out = []
        for (x, xb), (bsz, seq, tabs) in zip(carry, groups):
            out.append(_layer(x, xb, w, bias, tabs, bsz, seq))
        return tuple(out), None

    carry, _ = lax.scan(body, carry, w_all)
    return tuple(c[0].reshape(xin.shape) for c, xin in zip(carry, (x_prompt, x_sample)))
```

```python
import functools
import math

import jax
import jax.numpy as jnp
import numpy as np
from jax import lax
from jax.experimental import pallas as pl
from jax.experimental.pallas import tpu as pltpu

D_MODEL = 2048
DEPTH = 4
GRID_W = 64
BLOCK = 128
HEAD_DIM = 128
A_HEADS = 8
A_KV_HEADS = 2
WINDOW = 128
B_HEADS = 8
B_KV_HEADS = 2
C_HEADS = 8
C_Q_RANK = 512
C_KV_RANK = 512
C_NOPE = 128
C_ROPE = 64
C_V = 128
ROPE_THETA = 10000.0
N_BUCKETS = 32
MAX_DIST = 128
N_EXPERTS = 16
EXPERT_FF = 2048
EC_FACTOR = 2
N_BRANCH = 3
EPS = 1e-6
NEG = -1e30
LOG2E = 1.4426950408889634

LANE = 128
C_PAD = 2 * LANE
VMEM_LIMIT = 56 * 1024 * 1024

BF16 = jnp.bfloat16
F32 = jnp.float32


def _dn_alpha():
    return (2 * DEPTH) ** 0.25


def _layout():
    d = D_MODEL
    segs = [('g', N_BRANCH * d), ('qa', A_HEADS * HEAD_DIM), ('qb', B_HEADS * HEAD_DIM), ('cq', C_Q_RANK),
            ('ckv', C_KV_RANK), ('ka', A_KV_HEADS * HEAD_DIM), ('va', A_KV_HEADS * HEAD_DIM),
            ('kb', B_KV_HEADS * HEAD_DIM), ('vb', B_KV_HEADS * HEAD_DIM), ('kr', C_PAD)]
    off, o = {}, 0
    for name, w in segs:
        off[name] = (o, w)
        o += w
    total = -(-o // 512) * 512
    return off, total


def _cparams(sem, vmem=VMEM_LIMIT):
    return pltpu.CompilerParams(dimension_semantics=sem, vmem_limit_bytes=vmem)


def _mm_kernel(x_ref, w_ref, o_ref):
    o_ref[...] = jnp.dot(x_ref[...], w_ref[...], preferred_element_type=F32).astype(o_ref.dtype)


def _matmul(x, w, out_dtype, tm, tn):
    m, k = x.shape
    n = w.shape[1]
    tm, tn = min(tm, m), min(tn, n)
    return pl.pallas_call(
        _mm_kernel,
        grid=(m // tm, n // tn),
        in_specs=[pl.BlockSpec((tm, k), lambda i, j: (i, 0)), pl.BlockSpec((k, tn), lambda i, j: (0, j))],
        out_specs=pl.BlockSpec((tm, tn), lambda i, j: (i, j)),
        out_shape=jax.ShapeDtypeStruct((m, n), out_dtype),
        compiler_params=_cparams(("parallel", "parallel")),
        name="in_proj",
    )(x, w)


def _rope(x, cos, sa, sb):
    return x * cos + pltpu.roll(x, LANE - 32, 1) * sa + pltpu.roll(x, 32, 1) * sb


def _rms(x, g):
    return x * lax.rsqrt(jnp.mean(x * x, axis=-1, keepdims=True) + EPS) * g


def _bprep_kernel(q_ref, k_ref, cos_ref, sa_ref, sb_ref, gq_ref, gk_ref, qo_ref, ko_ref, *, scale):
    cos, sa, sb = cos_ref[...], sa_ref[...], sb_ref[...]
    gq, gk = gq_ref[...], gk_ref[...]
    for h in range(B_HEADS):
        x = q_ref[:, h * LANE:(h + 1) * LANE].astype(F32)
        qo_ref[:, h * LANE:(h + 1) * LANE] = (_rope(_rms(x, gq), cos, sa, sb) * scale).astype(BF16)
    for h in range(B_KV_HEADS):
        x = k_ref[:, h * LANE:(h + 1) * LANE].astype(F32)
        ko_ref[:, h * LANE:(h + 1) * LANE] = _rope(_rms(x, gk), cos, sa, sb).astype(BF16)


def _b_prep(h, tabs, gq, gk, seq, tm):
    off, _ = _layout()
    n = h.shape[0]
    tm = min(tm, seq)
    spb = seq // tm
    qw, kw = B_HEADS * HEAD_DIM, B_KV_HEADS * HEAD_DIM
    qo, ko = off['qb'][0] // qw, off['kb'][0] // kw
    tab_spec = pl.BlockSpec((tm, LANE), lambda i: (i % spb, 0))
    vec_spec = pl.BlockSpec((1, LANE), lambda i: (0, 0))
    scale = HEAD_DIM ** -0.5 * LOG2E
    return pl.pallas_call(
        functools.partial(_bprep_kernel, scale=scale),
        grid=(n // tm,),
        in_specs=[pl.BlockSpec((tm, qw), lambda i: (i, qo)), pl.BlockSpec((tm, kw), lambda i: (i, ko)),
                  tab_spec, tab_spec, tab_spec, vec_spec, vec_spec],
        out_specs=[pl.BlockSpec((tm, qw), lambda i: (i, 0)), pl.BlockSpec((tm, kw), lambda i: (i, 0))],
        out_shape=[jax.ShapeDtypeStruct((n, qw), BF16), jax.ShapeDtypeStruct((n, kw), BF16)],
        compiler_params=_cparams(("parallel",)),
        name="b_prep",
    )(h, h, *tabs, gq, gk)


def _cprep_kernel(cq_ref, ckv_ref, kr_ref, wq_ref, wk_ref, wv_ref, gq_ref, gkv_ref, cos_ref, sa_ref, sb_ref,
                  qo_ref, ko_ref, vo_ref, *, scale):
    cos, sa, sb = cos_ref[...], sa_ref[...], sb_ref[...]
    cqn = _rms(cq_ref[...].astype(F32), gq_ref[...]).astype(BF16)
    ckvn = _rms(ckv_ref[...].astype(F32), gkv_ref[...]).astype(BF16)
    q = jnp.dot(cqn, wq_ref[...], preferred_element_type=F32)
    kn = jnp.dot(ckvn, wk_ref[...], preferred_element_type=F32)
    vo_ref[...] = jnp.dot(ckvn, wv_ref[...], preferred_element_type=F32).astype(BF16)
    kr = _rope(kr_ref[:, LANE:].astype(F32), cos, sa, sb).astype(BF16)
    for h in range(C_HEADS):
        b = h * C_PAD
        qo_ref[:, b:b + LANE] = (q[:, b:b + LANE] * scale).astype(BF16)
        qo_ref[:, b + LANE:b + C_PAD] = (_rope(q[:, b + LANE:b + C_PAD], cos, sa, sb) * scale).astype(BF16)
        ko_ref[:, b:b + LANE] = kn[:, h * LANE:(h + 1) * LANE].astype(BF16)
        ko_ref[:, b + LANE:b + C_PAD] = kr


def _c_prep(h, wq, wk, wv, gq, gkv, tabs, seq, tm):
    off, _ = _layout()
    n = h.shape[0]
    tm = min(tm, seq)
    spb = seq // tm
    cqo, ckvo, kro = off['cq'][0] // C_Q_RANK, off['ckv'][0] // C_KV_RANK, off['kr'][0] // C_PAD
    tab_spec = pl.BlockSpec((tm, LANE), lambda i: (i % spb, 0))
    full = lambda a: pl.BlockSpec(a.shape, lambda i: (0, 0))
    scale = (C_NOPE + C_ROPE) ** -0.5 * LOG2E
    qw, vw = C_HEADS * C_PAD, C_HEADS * C_V
    return pl.pallas_call(
        functools.partial(_cprep_kernel, scale=scale),
        grid=(n // tm,),
        in_specs=[pl.BlockSpec((tm, C_Q_RANK), lambda i: (i, cqo)), pl.BlockSpec((tm, C_KV_RANK), lambda i: (i, ckvo)),
                  pl.BlockSpec((tm, C_PAD), lambda i: (i, kro)), full(wq), full(wk), full(wv), full(gq), full(gkv),
                  tab_spec, tab_spec, tab_spec],
        out_specs=[pl.BlockSpec((tm, qw), lambda i: (i, 0)), pl.BlockSpec((tm, qw), lambda i: (i, 0)),
                   pl.BlockSpec((tm, vw), lambda i: (i, 0))],
        out_shape=[jax.ShapeDtypeStruct((n, qw), BF16), jax.ShapeDtypeStruct((n, qw), BF16),
                   jax.ShapeDtypeStruct((n, vw), BF16)],
        compiler_params=_cparams(("parallel",)),
        name="c_prep",
    )(h, h, h, wq, wk, wv, gq, gkv, *tabs)


WIN_QB = 2
ONES_ROWS = 16


def _reduce_rows(x, op):
    acc = x[:8]
    for r in range(8, x.shape[0], 8):
        acc = op(acc, x[r:r + 8])
    while acc.shape[0] > 1:
        half = acc.shape[0] // 2
        acc = op(acc[:half], acc[half:])
    return acc[0]


def _win_kernel(q_ref, kp_ref, kc_ref, kn_ref, vtp_ref, vtc_ref, vtn_ref, bias_ref, sink_ref, o_ref, *, nb, scale):
    step = pl.program_id(1)
    g = A_HEADS // A_KV_HEADS
    ones = jnp.ones((ONES_ROWS, 3 * BLOCK), BF16)
    for u in range(WIN_QB):
        blk = step * WIN_QB + u
        edge = jnp.where(blk == 0, 1, jnp.where(blk == nb - 1, 2, 0))
        rs = slice(u * BLOCK, (u + 1) * BLOCK)
        for hk in range(A_KV_HEADS):
            ks = slice(hk * HEAD_DIM, (hk + 1) * HEAD_DIM)
            kparts = [kp_ref[:, ks]] + [kc_ref[w * BLOCK:(w + 1) * BLOCK, ks] for w in range(WIN_QB)] + [kn_ref[:, ks]]
            vtparts = [vtp_ref[0, hk, 0]] + [vtc_ref[0, hk, w] for w in range(WIN_QB)] + [vtn_ref[0, hk, 0]]
            k = jnp.concatenate(kparts[u:u + 3], axis=0)
            vt = jnp.concatenate([jnp.concatenate(vtparts[u:u + 3], axis=1), ones], axis=0)
            q = jnp.concatenate([q_ref[rs, (hk * g + j) * HEAD_DIM:(hk * g + j + 1) * HEAD_DIM] for j in range(g)],
                                axis=0)
            st = (lax.dot_general(k, q, (((1,), (1,)), ((), ())), preferred_element_type=F32) * scale
                  + bias_ref[edge, hk])
            sink = sink_ref[hk]
            colmax = jnp.max(_reduce_rows(st.reshape(3 * BLOCK // 8, 8, g * BLOCK), jnp.maximum), axis=0, keepdims=True)
            m = jnp.maximum(colmax, sink)
            p = jnp.exp(st - m).astype(BF16)
            acc = jnp.dot(vt, p, preferred_element_type=F32)
            denom = acc[HEAD_DIM:HEAD_DIM + 1] + jnp.exp(sink - m)
            o = (acc[:HEAD_DIM] / denom).T.astype(BF16)
            for j in range(g):
                o_ref[rs, (hk * g + j) * HEAD_DIM:(hk * g + j + 1) * HEAD_DIM] = o[j * BLOCK:(j + 1) * BLOCK]


def _window_attention(h, vta, bias, sink, bsz, seq):
    off, _ = _layout()
    nb = seq // BLOCK
    assert nb % WIN_QB == 0 and nb >= 2
    ns = nb // WIN_QB
    qw, kw = A_HEADS * HEAD_DIM, A_KV_HEADS * HEAD_DIM
    qo, ko = off['qa'][0] // qw, off['ka'][0] // kw
    cur = lambda b, n: b * ns + n
    prv = lambda n: jnp.maximum(n * WIN_QB - 1, 0)
    nxt = lambda n: jnp.minimum((n + 1) * WIN_QB, nb - 1)
    kedge = lambda f: pl.BlockSpec((BLOCK, kw), lambda b, n: (b * nb + f(n), ko))
    vedge = lambda f: pl.BlockSpec((1, A_KV_HEADS, 1, HEAD_DIM, BLOCK), lambda b, n: (b, 0, f(n), 0, 0))
    full = lambda a: pl.BlockSpec(a.shape, lambda b, n: (0,) * a.ndim)
    return pl.pallas_call(
        functools.partial(_win_kernel, nb=nb, scale=HEAD_DIM ** -0.5),
        grid=(bsz, ns),
        in_specs=[pl.BlockSpec((WIN_QB * BLOCK, qw), lambda b, n: (cur(b, n), qo)),
                  kedge(prv), pl.BlockSpec((WIN_QB * BLOCK, kw), lambda b, n: (cur(b, n), ko)), kedge(nxt),
                  vedge(prv), pl.BlockSpec((1, A_KV_HEADS, WIN_QB, HEAD_DIM, BLOCK), lambda b, n: (b, 0, n, 0, 0)),
                  vedge(nxt), full(bias), full(sink)],
        out_specs=pl.BlockSpec((WIN_QB * BLOCK, qw), lambda b, n: (cur(b, n), 0)),
        out_shape=jax.ShapeDtypeStruct((bsz * seq, qw), BF16),
        compiler_params=_cparams(("parallel", "parallel")),
        name="window_attn",
    )(h, h, h, h, vta, vta, vta, bias, sink)


Q_CHUNK = 2 * LANE
KV_SUB = 512
KV_UNROLL = 4


def _flash_kernel(q_ref, k_ref, vt_ref, o_ref, m_sc, acc_sc, s0_sc, s1_sc, mx0_sc, mx1_sc, *,
                  nsub, nchunk, g, dq, dv):
    ki = pl.program_id(3)

    @pl.when(ki == 0)
    def _():
        m_sc[...] = jnp.full(m_sc.shape, -jnp.inf, F32)
        acc_sc[...] = jnp.zeros(acc_sc.shape, F32)

    def q_chunk(j):
        return q_ref[:, j * dq:(j + 1) * dq] if g > 1 else q_ref[j * Q_CHUNK:(j + 1) * Q_CHUNK, :]

    def scores(t, s_ref, mx_ref):
        k = k_ref[pl.ds(pl.multiple_of(t * KV_SUB, KV_SUB), KV_SUB), :]
        for j in range(nchunk):
            cs = slice(j * Q_CHUNK, (j + 1) * Q_CHUNK)
            st = lax.dot_general(k, q_chunk(j), (((1,), (1,)), ((), ())), preferred_element_type=F32)
            s_ref[:, cs] = st
            mx_ref[:, cs] = _reduce_rows(st.reshape(KV_SUB // 8, 8, Q_CHUNK), jnp.maximum)

    def softmax_pv(t, s_ref, mx_ref):
        vt = jnp.concatenate([vt_ref[0, 0, t], jnp.ones((ONES_ROWS, KV_SUB), BF16)], axis=0)
        for j in range(nchunk):
            cs = slice(j * Q_CHUNK, (j + 1) * Q_CHUNK)
            m_prev = m_sc[:, cs]
            m_new = jnp.maximum(m_prev, jnp.max(mx_ref[:, cs], axis=0, keepdims=True))
            alpha = jnp.exp2(m_prev - m_new)
            p = jnp.exp2(s_ref[:, cs] - m_new).astype(BF16)
            acc_sc[:, cs] = alpha * acc_sc[:, cs] + jnp.dot(vt, p, preferred_element_type=F32)
            m_sc[:, cs] = m_new

    bufs = ((s0_sc, mx0_sc), (s1_sc, mx1_sc))
    scores(0, *bufs[0])

    def body(i, carry):
        t = KV_UNROLL * i
        for u in range(KV_UNROLL):
            scores(t + u + 1, *bufs[(u + 1) % 2])
            softmax_pv(t + u, *bufs[u % 2])
        return carry

    trips = (nsub - 1) // KV_UNROLL
    lax.fori_loop(0, trips, body, 0)
    for t in range(trips * KV_UNROLL, nsub):
        if t + 1 < nsub:
            scores(t + 1, *bufs[(t + 1) % 2])
        softmax_pv(t, *bufs[t % 2])

    @pl.when(ki == pl.num_programs(3) - 1)
    def _():
        for j in range(nchunk):
            cs = slice(j * Q_CHUNK, (j + 1) * Q_CHUNK)
            o = (acc_sc[:dv, cs] / acc_sc[dv:dv + 1, cs]).T.astype(o_ref.dtype)
            if g > 1:
                o_ref[:, j * dv:(j + 1) * dv] = o
            else:
                o_ref[j * Q_CHUNK:(j + 1) * Q_CHUNK, :] = o


def _kv_major_t(v, bsz, seq, hk, dv):
    sub = min(KV_SUB, seq)
    return v.reshape(bsz, seq // sub, sub, hk, dv).transpose(0, 3, 1, 4, 2)


def _flash(q, k, vt, bsz, seq, hq, hk, dq, dv, bk):
    g = hq // hk
    assert seq % KV_SUB == 0 and (g == 1 or g * Q_CHUNK <= 4 * Q_CHUNK)
    bk = min(bk, seq)
    nchunk = g if g > 1 else 4
    bq = Q_CHUNK if g > 1 else nchunk * Q_CHUNK
    assert seq % bq == 0 and seq % bk == 0 and bk % (2 * KV_SUB) == 0
    nq, nk, nsub = seq // bq, seq // bk, bk // KV_SUB
    return pl.pallas_call(
        functools.partial(_flash_kernel, nsub=nsub, nchunk=nchunk, g=g, dq=dq, dv=dv),
        grid=(bsz, hk, nq, nk),
        in_specs=[pl.BlockSpec((bq, g * dq), lambda b, h, i, j: (b * nq + i, h)),
                  pl.BlockSpec((bk, dq), lambda b, h, i, j: (b * nk + j, h)),
                  pl.BlockSpec((1, 1, nsub, dv, KV_SUB), lambda b, h, i, j: (b, h, j, 0, 0))],
        out_specs=pl.BlockSpec((bq, g * dv), lambda b, h, i, j: (b * nq + i, h)),
        out_shape=jax.ShapeDtypeStruct((bsz * seq, hq * dv), BF16),
        scratch_shapes=[pltpu.VMEM((1, nchunk * Q_CHUNK), F32),
                        pltpu.VMEM((dv + ONES_ROWS, nchunk * Q_CHUNK), F32), pltpu.VMEM((KV_SUB, nchunk * Q_CHUNK), F32),
                        pltpu.VMEM((KV_SUB, nchunk * Q_CHUNK), F32), pltpu.VMEM((8, nchunk * Q_CHUNK), F32),
                        pltpu.VMEM((8, nchunk * Q_CHUNK), F32)],
        compiler_params=_cparams(("parallel", "parallel", "parallel", "arbitrary")),
        name="dense_attn",
    )(q, k, vt)


def _merge_kernel(oa_ref, ob_ref, oc_ref, wa_ref, wb_ref, wc_ref, ga_ref, gb_ref, gc_ref, ba_ref, bb_ref, bc_ref,
                  o_ref):
    def branch(o_r, w_r, g_r, b_r):
        gate = jax.nn.sigmoid(g_r[...].astype(F32) + b_r[...])
        return gate * jnp.dot(o_r[...], w_r[...], preferred_element_type=F32)

    o_ref[...] = (branch(oa_ref, wa_ref, ga_ref, ba_ref) + branch(ob_ref, wb_ref, gb_ref, bb_ref)
                  + branch(oc_ref, wc_ref, gc_ref, bc_ref)).astype(o_ref.dtype)


def _merge(oa, ob, oc, wa, wb, wc, h, b_gate, tm, tn):
    n, kdim = oa.shape
    d = D_MODEL
    tm, tn = min(tm, n), min(tn, d)
    nj = d // tn
    act = pl.BlockSpec((tm, kdim), lambda i, j: (i, 0))
    wsp = pl.BlockSpec((kdim, tn), lambda i, j: (0, j))
    gate = lambda br: pl.BlockSpec((tm, tn), lambda i, j: (i, br * nj + j))
    bias = lambda br: pl.BlockSpec((1, tn), lambda i, j: (0, br * nj + j))
    return pl.pallas_call(
        _merge_kernel,
        grid=(n // tm, nj),
        in_specs=[act, act, act, wsp, wsp, wsp, gate(0), gate(1), gate(2), bias(0), bias(1), bias(2)],
        out_specs=pl.BlockSpec((tm, tn), lambda i, j: (i, j)),
        out_shape=jax.ShapeDtypeStruct((n, d), BF16),
        compiler_params=_cparams(("parallel", "parallel")),
        name="branch_merge",
    )(oa, ob, oc, wa, wb, wc, h, h, h, b_gate, b_gate, b_gate)


def _layer_norm(z, g, b):
    zc = z - jnp.mean(z, axis=-1, keepdims=True)
    var = jnp.mean(zc * zc, axis=-1, keepdims=True)
    return zc * lax.rsqrt(var + EPS) * g + b


def _woln_kernel(m_ref, wo_ref, x_ref, g_ref, b_ref, wr_ref, x1_ref, acc_ref, lg_ref, *, alpha):
    y = jnp.dot(m_ref[...], wo_ref[...], preferred_element_type=F32)
    x1 = _layer_norm(alpha * x_ref[...] + y, g_ref[...], b_ref[...])
    x1_ref[...] = x1
    acc_ref[...] = alpha * x1
    lg_ref[...] = jnp.dot(x1.astype(BF16), wr_ref[...], preferred_element_type=F32)


def _wo_ln(merged, wo, x, g, b, wr, tm):
    n, d = x.shape
    tm = min(tm, n)
    row = lambda w: pl.BlockSpec((tm, w), lambda i: (i, 0))
    full = lambda a: pl.BlockSpec(a.shape, lambda i: (0, 0))
    return pl.pallas_call(
        functools.partial(_woln_kernel, alpha=_dn_alpha()),
        grid=(n // tm,),
        in_specs=[row(d), full(wo), row(d), full(g), full(b), full(wr)],
        out_specs=[row(d), row(d), row(LANE)],
        out_shape=[jax.ShapeDtypeStruct((n, d), F32), jax.ShapeDtypeStruct((n, d), F32),
                   jax.ShapeDtypeStruct((n, LANE), F32)],
        compiler_params=_cparams(("parallel",)),
        name="wo_ln_router",
    )(merged, wo, x, g, b, wr)


def _route_kernel(lg_ref, aff_ref, thr_ref, ngt_ref, *, cap):
    lg = lg_ref[...]
    e = jnp.exp(lg - jnp.max(lg, axis=0, keepdims=True))
    aff = e / jnp.sum(e, axis=0, keepdims=True)
    aff_ref[...] = aff
    bits = lax.bitcast_convert_type(aff, jnp.int32)

    def step(i, t):
        cand = t | (jnp.int32(1) << (30 - i))
        cnt = jnp.sum((bits >= cand).astype(jnp.int32), axis=1, keepdims=True)
        return jnp.where(cnt >= cap, cand, t)

    thr = lax.fori_loop(0, 31, step, jnp.zeros((lg.shape[0], 1), jnp.int32))
    ngt = jnp.sum((bits > thr).astype(jnp.int32), axis=1, keepdims=True)
    thr_ref[...] = jnp.broadcast_to(thr, thr_ref.shape)
    ngt_ref[...] = jnp.broadcast_to(ngt, ngt_ref.shape)


def _route(logits_t, cap):
    e, n = logits_t.shape
    full = lambda s: pl.BlockSpec(s, lambda: (0, 0))
    return pl.pallas_call(
        functools.partial(_route_kernel, cap=cap),
        in_specs=[full((e, n))],
        out_specs=[full((e, n)), full((e, LANE)), full((e, LANE))],
        out_shape=[jax.ShapeDtypeStruct((e, n), F32), jax.ShapeDtypeStruct((e, LANE), jnp.int32),
                   jax.ShapeDtypeStruct((e, LANE), jnp.int32)],
        compiler_params=_cparams(None),
        name="route_select",
    )(logits_t)


def _select_indices(aff, thr, ngt, cap):
    e, n = aff.shape
    bits = lax.bitcast_convert_type(aff, jnp.int32)
    t, g = thr[:, :1], ngt[:, :1]
    gt, eq = bits > t, bits == t
    sel = gt | (eq & (jnp.cumsum(eq.astype(jnp.int32), axis=1) <= cap - g))
    tok = jnp.arange(n, dtype=jnp.int32)[None]
    idx, gval = lax.sort((jnp.where(sel, tok, tok + n), aff), dimension=1, num_keys=1)
    return idx[:, :cap], gval[:, :cap]


MOE_DMA_UNROLL = 8


def _moe_kernel(idx_ref, nxt_ref, x_hbm, acc_in, gv_ref, wg_ref, wu_ref, wd_ref, acc_out, xbuf, obuf, xb, hacc,
                sem_x, sem_o, sem_w, *, tm):
    e, i, f = pl.program_id(0), pl.program_id(1), pl.program_id(2)
    nf = pl.num_programs(2)
    tile = e * pl.num_programs(1) + i
    last_tile = pl.num_programs(0) * pl.num_programs(1) - 1

    def rows(ids_ref, make_copy):
        def issue(r0, c):
            for u in range(MOE_DMA_UNROLL):
                r = r0 * MOE_DMA_UNROLL + u
                make_copy(ids_ref[0, 0, r], r).start()
            return c

        lax.fori_loop(0, tm // MOE_DMA_UNROLL, issue, 0)

    x_row = lambda t, r: pltpu.make_async_copy(x_hbm.at[pl.ds(t, 1)], xbuf.at[pl.ds(r, 1)], sem_x)
    acc_row = lambda t, r: pltpu.make_async_copy(acc_in.at[pl.ds(t, 1)], obuf.at[pl.ds(r, 1)], sem_o)
    out_row = lambda t, r: pltpu.make_async_copy(obuf.at[pl.ds(r, 1)], acc_out.at[pl.ds(t, 1)], sem_w)
    wait_x = lambda: pltpu.make_async_copy(x_hbm.at[pl.ds(0, tm)], xbuf, sem_x).wait()
    wait_acc = lambda: pltpu.make_async_copy(acc_in.at[pl.ds(0, tm)], obuf, sem_o).wait()
    wait_out = lambda: pltpu.make_async_copy(obuf, acc_out.at[pl.ds(0, tm)], sem_w).wait()

    @pl.when((f == 0) & (tile == 0))
    def _():
        rows(idx_ref, x_row)

    @pl.when(f == 0)
    def _():
        wait_x()
        xb[...] = xbuf[...].astype(BF16)
        hacc[...] = jnp.zeros(hacc.shape, F32)

    @pl.when((f == 1) & (tile > 0))
    def _():
        wait_out()

    @pl.when(f == 1)
    def _():
        rows(idx_ref, acc_row)

    @pl.when((f == 2) & (tile < last_tile))
    def _():
        rows(nxt_ref, x_row)

    x = xb[...]
    gate = jnp.dot(x, wg_ref[0], preferred_element_type=F32)
    up = jnp.dot(x, wu_ref[0], preferred_element_type=F32)
    hh = (gate * jax.nn.sigmoid(gate) * up).astype(BF16)
    hacc[...] += jnp.dot(hh, wd_ref[0], preferred_element_type=F32)

    @pl.when(f == nf - 1)
    def _():
        wait_acc()
        obuf[...] = obuf[...] + hacc[...] * gv_ref[0]
        rows(idx_ref, out_row)

    @pl.when((f == nf - 1) & (tile == last_tile))
    def _():
        wait_out()


def _moe_ffn(x1, acc0, idx, gval, wg, wu, wd, tm, tf):
    n, d = x1.shape
    e, cap = idx.shape
    ff = wg.shape[2]
    tm, tf = min(tm, cap), min(tf, ff)
    nt, nf = cap // tm, ff // tf
    assert nf >= 4 and tm % MOE_DMA_UNROLL == 0
    idx3 = idx.reshape(e * nt, 1, tm)
    gv3 = gval.reshape(e * nt, tm, 1)
    anyspec = pl.BlockSpec(memory_space=pl.ANY)
    last = e * nt - 1
    return pl.pallas_call(
        functools.partial(_moe_kernel, tm=tm),
        grid=(e, nt, nf),
        in_specs=[pl.BlockSpec((1, 1, tm), lambda ei, i, f: (ei * nt + i, 0, 0), memory_space=pltpu.SMEM),
                  pl.BlockSpec((1, 1, tm), lambda ei, i, f: (jnp.minimum(ei * nt + i + 1, last), 0, 0),
                               memory_space=pltpu.SMEM),
                  anyspec, anyspec,
                  pl.BlockSpec((1, tm, 1), lambda ei, i, f: (ei * nt + i, 0, 0)),
                  pl.BlockSpec((1, d, tf), lambda ei, i, f: (ei, 0, f)),
                  pl.BlockSpec((1, d, tf), lambda ei, i, f: (ei, 0, f)),
                  pl.BlockSpec((1, tf, d), lambda ei, i, f: (ei, f, 0))],
        out_specs=anyspec,
        out_shape=jax.ShapeDtypeStruct((n, d), F32),
        scratch_shapes=[pltpu.VMEM((tm, d), F32), pltpu.VMEM((tm, d), F32), pltpu.VMEM((tm, d), BF16),
                        pltpu.VMEM((tm, d), F32), pltpu.SemaphoreType.DMA(()), pltpu.SemaphoreType.DMA(()),
                        pltpu.SemaphoreType.DMA(())],
        input_output_aliases={3: 0},
        compiler_params=_cparams(("arbitrary", "arbitrary", "arbitrary")),
        name="moe_ffn",
    )(idx3, idx3, x1, acc0, gv3, wg, wu, wd)


def _ln_kernel(z_ref, g_ref, b_ref, x_ref, xb_ref):
    x = _layer_norm(z_ref[...], g_ref[...], b_ref[...])
    x_ref[...] = x
    xb_ref[...] = x.astype(BF16)


def _ln_out(z, g, b, tm):
    n, d = z.shape
    tm = min(tm, n)
    row = pl.BlockSpec((tm, d), lambda i: (i, 0))
    vec = pl.BlockSpec((1, d), lambda i: (0, 0))
    return pl.pallas_call(
        _ln_kernel,
        grid=(n // tm,),
        in_specs=[row, vec, vec],
        out_specs=[row, row],
        out_shape=[jax.ShapeDtypeStruct((n, d), F32), jax.ShapeDtypeStruct((n, d), BF16)],
        compiler_params=_cparams(("parallel",)),
        name="ln_out",
    )(z, g, b)


def _t5_bucket(rel):
    nb = N_BUCKETS // 2
    max_exact = nb // 2
    n = jnp.abs(rel)
    scaled = jnp.log(jnp.maximum(n, 1).astype(F32) / max_exact) / math.log(MAX_DIST / max_exact)
    large = jnp.minimum(max_exact + (scaled * (nb - max_exact)).astype(jnp.int32), nb - 1)
    return jnp.where(rel > 0, nb, 0) + jnp.where(n < max_exact, n, large)


def _rope_tabs(ang_first, ang_second):
    def half(ang):
        if ang is None:
            return None
        return jnp.concatenate([ang, ang], axis=-1)
    a1, a2 = half(ang_first), half(ang_second)
    s = a1.shape[0]
    if a2 is None:
        cos = jnp.concatenate([jnp.cos(a1), jnp.ones((s, 64), F32)], -1)
        sin = jnp.concatenate([jnp.sin(a1), jnp.zeros((s, 64), F32)], -1)
    else:
        cos = jnp.concatenate([jnp.cos(a1), jnp.cos(a2)], -1)
        sin = jnp.concatenate([jnp.sin(a1), jnp.sin(a2)], -1)
    first = (jnp.arange(LANE) % 64) < 32
    return cos, jnp.where(first, -sin, 0.0), jnp.where(first, 0.0, sin)


def _tables(seq):
    rows = seq // GRID_W
    row = jnp.repeat(jnp.arange(rows), GRID_W).astype(F32)
    col = jnp.tile(jnp.arange(GRID_W), rows).astype(F32)
    t = jnp.arange(seq).astype(F32)
    inv_ax = ROPE_THETA ** (-jnp.arange(0, HEAD_DIM // 2, 2, dtype=F32) / (HEAD_DIM // 2))
    inv_t = ROPE_THETA ** (-jnp.arange(0, C_ROPE, 2, dtype=F32) / C_ROPE)
    ax = _rope_tabs(row[:, None] * inv_ax[None], col[:, None] * inv_ax[None])
    tt = _rope_tabs(t[:, None] * inv_t[None], None)
    return ax, tt


def _prep_weights(p):
    off, total = _layout()
    d = D_MODEL
    L = p['w_in'].shape[0]
    splits = np.cumsum([A_HEADS * HEAD_DIM, A_KV_HEADS * HEAD_DIM, A_KV_HEADS * HEAD_DIM, B_HEADS * HEAD_DIM,
                        B_KV_HEADS * HEAD_DIM, B_KV_HEADS * HEAD_DIM, C_Q_RANK, C_KV_RANK, C_ROPE])
    qa, ka, va, qb, kb, vb, cq, ckv, kr, g = jnp.split(p['w_in'].astype(BF16), [int(s) for s in splits], axis=-1)
    z = lambda w: jnp.zeros((L, d, w), BF16)
    krp = jnp.concatenate([z(LANE), kr, z(C_PAD - LANE - C_ROPE)], -1)
    cols = dict(g=g, qa=qa, qb=qb, cq=cq, ckv=ckv, ka=ka, va=va, kb=kb, vb=vb, kr=krp)
    parts, o = [], 0
    for name, (start, width) in off.items():
        assert start == o and cols[name].shape[-1] == width
        parts.append(cols[name])
        o += width
    parts.append(z(total - o))
    w_in = jnp.concatenate(parts, -1)

    wuq = p['w_uq'].astype(BF16).reshape(L, C_Q_RANK, C_HEADS, C_NOPE + C_ROPE)
    wuq = jnp.pad(wuq, ((0, 0), (0, 0), (0, 0), (0, C_PAD - C_NOPE - C_ROPE))).reshape(L, C_Q_RANK, C_HEADS * C_PAD)
    wukv = p['w_ukv'].astype(BF16).reshape(L, C_KV_RANK, C_HEADS, C_NOPE + C_V)
    wk = wukv[..., :C_NOPE].reshape(L, C_KV_RANK, C_HEADS * C_NOPE)
    wv = wukv[..., C_NOPE:].reshape(L, C_KV_RANK, C_HEADS * C_V)
    wr = jnp.pad(p['w_router'], ((0, 0), (0, 0), (0, LANE - N_EXPERTS))).astype(BF16)
    row = lambda a: a[:, None, :]
    g_heads = A_HEADS // A_KV_HEADS
    sink = jnp.broadcast_to(p['a_sink'].reshape(L, A_KV_HEADS, 1, g_heads, 1),
                            (L, A_KV_HEADS, 1, g_heads, BLOCK)).reshape(L, A_KV_HEADS, 1, g_heads * BLOCK)
    return dict(w_in=w_in, b_gate=row(p['b_gate']), gqb=row(p['b_q_norm']), gkb=row(p['b_k_norm']),
                gcq=row(p['c_q_norm']), gckv=row(p['c_kv_norm']), wuq=wuq, wk=wk, wv=wv, sink=sink,
                wa=p['w_branch_a'].astype(BF16), wb=p['w_branch_b'].astype(BF16), wc=p['w_branch_c'].astype(BF16),
                wo=p['w_o'].astype(BF16), ln1_g=row(p['ln1_g']), ln1_b=row(p['ln1_b']), wr=wr,
                weg=p['w_e_gate'].astype(BF16), weu=p['w_e_up'].astype(BF16), wed=p['w_e_down'].astype(BF16),
                ln2_g=row(p['ln2_g']), ln2_b=row(p['ln2_b']))


def _layer(x, xb, w, bias, tabs, bsz, seq):
    off, _ = _layout()
    n = bsz * seq
    ax_tabs, t_tabs = tabs
    h = _matmul(xb, w['w_in'], BF16, 2048, 512)
    va = h[:, off['va'][0]:off['va'][0] + off['va'][1]]
    vta = va.reshape(bsz, seq // BLOCK, BLOCK, A_KV_HEADS, HEAD_DIM).transpose(0, 3, 1, 4, 2)
    oa = _window_attention(h, vta, bias, w['sink'], bsz, seq)
    qb, kb = _b_prep(h, ax_tabs, w['gqb'], w['gkb'], seq, 512)
    vb = h[:, off['vb'][0]:off['vb'][0] + off['vb'][1]]
    ob = _flash(qb, kb, _kv_major_t(vb, bsz, seq, B_KV_HEADS, HEAD_DIM), bsz, seq, B_HEADS, B_KV_HEADS, HEAD_DIM,
                HEAD_DIM, 16384)
    qc, kc, vc = _c_prep(h, w['wuq'], w['wk'], w['wv'], w['gcq'], w['gckv'], t_tabs, seq, 512)
    oc = _flash(qc, kc, _kv_major_t(vc, bsz, seq, C_HEADS, C_V), bsz, seq, C_HEADS, C_HEADS, C_PAD, C_V, 16384)
    merged = _merge(oa, ob, oc, w['wa'], w['wb'], w['wc'], h, w['b_gate'], 1024, 1024)
    x1, acc0, logits = _wo_ln(merged, w['wo'], x, w['ln1_g'], w['ln1_b'], w['wr'], 512)
    cap = EC_FACTOR * n // N_EXPERTS
    aff, thr, ngt = _route(logits[:, :N_EXPERTS].T, cap)
    idx, gval = _select_indices(aff, thr, ngt, cap)
    acc = _moe_ffn(x1, acc0, idx, gval, w['weg'], w['weu'], w['wed'], 1024, 512)
    return _ln_out(acc, w['ln2_g'], w['ln2_b'], 512)


def _rel_bias(rel_table):
    offs = jnp.arange(3 * BLOCK)[None, :] - BLOCK - jnp.arange(BLOCK)[:, None]
    rb = jnp.transpose(rel_table[_t5_bucket(offs)], (2, 0, 1)).astype(F32)
    rb = jnp.where(jnp.abs(offs)[None] <= WINDOW, rb, NEG)
    col = jnp.arange(3 * BLOCK)
    variants = jnp.stack([rb, jnp.where(col >= BLOCK, rb, NEG), jnp.where(col < 2 * BLOCK, rb, NEG)])
    g = A_HEADS // A_KV_HEADS
    return variants.reshape(3, A_KV_HEADS, g * BLOCK, 3 * BLOCK).transpose(0, 1, 3, 2)


def kernel(x_prompt, x_sample, rel_table, w_in, b_gate, b_q_norm, b_k_norm, c_q_norm, c_kv_norm, w_uq, w_ukv, a_sink,
           w_branch_a, w_branch_b, w_branch_c, w_o, ln1_g, ln1_b, w_router, w_e_gate, w_e_up, w_e_down, ln2_g, ln2_b):
    params = dict(w_in=w_in, b_gate=b_gate, b_q_norm=b_q_norm, b_k_norm=b_k_norm, c_q_norm=c_q_norm,
                  c_kv_norm=c_kv_norm, w_uq=w_uq, w_ukv=w_ukv, a_sink=a_sink, w_branch_a=w_branch_a,
                  w_branch_b=w_branch_b, w_branch_c=w_branch_c, w_o=w_o, ln1_g=ln1_g, ln1_b=ln1_b,
                  w_router=w_router, w_e_gate=w_e_gate, w_e_up=w_e_up, w_e_down=w_e_down, ln2_g=ln2_g, ln2_b=ln2_b)
    w_all = _prep_weights(params)
    bias = _rel_bias(rel_table)
    groups = []
    for x in (x_prompt, x_sample):
        bsz, seq, d = x.shape
        groups.append((bsz, seq, _tables(seq)))
    xs = tuple(x.reshape(-1, x.shape[-1]) for x in (x_prompt, x_sample))
    carry = tuple((x, x.astype(BF16)) for x in xs)

    def body(carry, w):
        out = []
        for (x, xb), (bsz, seq, tabs) in zip(carry, groups):
            out.append(_layer(x, xb, w, bias, tabs, bsz, seq))
        return tuple(out), None

    carry, _ = lax.scan(body, carry, w_all)
    return tuple(c[0].reshape(xin.shape) for c, xin in zip(carry, (x_prompt, x_sample)))
```

```python
import functools
import math

import jax
import jax.numpy as jnp
import numpy as np
from jax import lax
from jax.experimental import pallas as pl
from jax.experimental.pallas import tpu as pltpu

D_MODEL = 2048
DEPTH = 4
GRID_W = 64
BLOCK = 128
HEAD_DIM = 128
A_HEADS = 8
A_KV_HEADS = 2
WINDOW = 128
B_HEADS = 8
B_KV_HEADS = 2
C_HEADS = 8
C_Q_RANK = 512
C_KV_RANK = 512
C_NOPE = 128
C_ROPE = 64
C_V = 128
ROPE_THETA = 10000.0
N_BUCKETS = 32
MAX_DIST = 128
N_EXPERTS = 16
EXPERT_FF = 2048
EC_FACTOR = 2
N_BRANCH = 3
EPS = 1e-6
NEG = -1e30
LOG2E = 1.4426950408889634

LANE = 128
C_PAD = 2 * LANE
VMEM_LIMIT = 56 * 1024 * 1024

BF16 = jnp.bfloat16
F32 = jnp.float32


def _dn_alpha():
    return (2 * DEPTH) ** 0.25


def _layout():
    d = D_MODEL
    segs = [('g', N_BRANCH * d), ('qa', A_HEADS * HEAD_DIM), ('qb', B_HEADS * HEAD_DIM), ('cq', C_Q_RANK),
            ('ckv', C_KV_RANK), ('ka', A_KV_HEADS * HEAD_DIM), ('va', A_KV_HEADS * HEAD_DIM),
            ('kb', B_KV_HEADS * HEAD_DIM), ('vb', B_KV_HEADS * HEAD_DIM), ('kr', C_PAD)]
    off, o = {}, 0
    for name, w in segs:
        off[name] = (o, w)
        o += w
    total = -(-o // 512) * 512
    return off, total


def _cparams(sem, vmem=VMEM_LIMIT):
    return pltpu.CompilerParams(dimension_semantics=sem, vmem_limit_bytes=vmem)


def _mm_kernel(x_ref, w_ref, o_ref):
    o_ref[...] = jnp.dot(x_ref[...], w_ref[...], preferred_element_type=F32).astype(o_ref.dtype)


def _matmul(x, w, out_dtype, tm, tn):
    m, k = x.shape
    n = w.shape[1]
    tm, tn = min(tm, m), min(tn, n)
    return pl.pallas_call(
        _mm_kernel,
        grid=(m // tm, n // tn),
        in_specs=[pl.BlockSpec((tm, k), lambda i, j: (i, 0)), pl.BlockSpec((k, tn), lambda i, j: (0, j))],
        out_specs=pl.BlockSpec((tm, tn), lambda i, j: (i, j)),
        out_shape=jax.ShapeDtypeStruct((m, n), out_dtype),
        compiler_params=_cparams(("parallel", "parallel")),
        name="in_proj",
    )(x, w)


def _rope(x, cos, sa, sb):
    return x * cos + pltpu.roll(x, LANE - 32, 1) * sa + pltpu.roll(x, 32, 1) * sb


def _rms(x, g):
    return x * lax.rsqrt(jnp.mean(x * x, axis=-1, keepdims=True) + EPS) * g


def _bprep_kernel(q_ref, k_ref, cos_ref, sa_ref, sb_ref, gq_ref, gk_ref, qo_ref, ko_ref, *, scale):
    cos, sa, sb = cos_ref[...], sa_ref[...], sb_ref[...]
    gq, gk = gq_ref[...], gk_ref[...]
    for h in range(B_HEADS):
        x = q_ref[:, h * LANE:(h + 1) * LANE].astype(F32)
        qo_ref[:, h * LANE:(h + 1) * LANE] = (_rope(_rms(x, gq), cos, sa, sb) * scale).astype(BF16)
    for h in range(B_KV_HEADS):
        x = k_ref[:, h * LANE:(h + 1) * LANE].astype(F32)
        ko_ref[:, h * LANE:(h + 1) * LANE] = _rope(_rms(x, gk), cos, sa, sb).astype(BF16)


def _b_prep(h, tabs, gq, gk, seq, tm):
    off, _ = _layout()
    n = h.shape[0]
    tm = min(tm, seq)
    spb = seq // tm
    qw, kw = B_HEADS * HEAD_DIM, B_KV_HEADS * HEAD_DIM
    qo, ko = off['qb'][0] // qw, off['kb'][0] // kw
    tab_spec = pl.BlockSpec((tm, LANE), lambda i: (i % spb, 0))
    vec_spec = pl.BlockSpec((1, LANE), lambda i: (0, 0))
    scale = HEAD_DIM ** -0.5 * LOG2E
    return pl.pallas_call(
        functools.partial(_bprep_kernel, scale=scale),
        grid=(n // tm,),
        in_specs=[pl.BlockSpec((tm, qw), lambda i: (i, qo)), pl.BlockSpec((tm, kw), lambda i: (i, ko)),
                  tab_spec, tab_spec, tab_spec, vec_spec, vec_spec],
        out_specs=[pl.BlockSpec((tm, qw), lambda i: (i, 0)), pl.BlockSpec((tm, kw), lambda i: (i, 0))],
        out_shape=[jax.ShapeDtypeStruct((n, qw), BF16), jax.ShapeDtypeStruct((n, kw), BF16)],
        compiler_params=_cparams(("parallel",)),
        name="b_prep",
    )(h, h, *tabs, gq, gk)


def _cprep_kernel(cq_ref, ckv_ref, kr_ref, wq_ref, wk_ref, wv_ref, gq_ref, gkv_ref, cos_ref, sa_ref, sb_ref,
                  qo_ref, ko_ref, vo_ref, *, scale):
    cos, sa, sb = cos_ref[...], sa_ref[...], sb_ref[...]
    cqn = _rms(cq_ref[...].astype(F32), gq_ref[...]).astype(BF16)
    ckvn = _rms(ckv_ref[...].astype(F32), gkv_ref[...]).astype(BF16)
    q = jnp.dot(cqn, wq_ref[...], preferred_element_type=F32)
    kn = jnp.dot(ckvn, wk_ref[...], preferred_element_type=F32)
    vo_ref[...] = jnp.dot(ckvn, wv_ref[...], preferred_element_type=F32).astype(BF16)
    kr = _rope(kr_ref[:, LANE:].astype(F32), cos, sa, sb).astype(BF16)
    for h in range(C_HEADS):
        b = h * C_PAD
        qo_ref[:, b:b + LANE] = (q[:, b:b + LANE] * scale).astype(BF16)
        qo_ref[:, b + LANE:b + C_PAD] = (_rope(q[:, b + LANE:b + C_PAD], cos, sa, sb) * scale).astype(BF16)
        ko_ref[:, b:b + LANE] = kn[:, h * LANE:(h + 1) * LANE].astype(BF16)
        ko_ref[:, b + LANE:b + C_PAD] = kr


def _c_prep(h, wq, wk, wv, gq, gkv, tabs, seq, tm):
    off, _ = _layout()
    n = h.shape[0]
    tm = min(tm, seq)
    spb = seq // tm
    cqo, ckvo, kro = off['cq'][0] // C_Q_RANK, off['ckv'][0] // C_KV_RANK, off['kr'][0] // C_PAD
    tab_spec = pl.BlockSpec((tm, LANE), lambda i: (i % spb, 0))
    full = lambda a: pl.BlockSpec(a.shape, lambda i: (0, 0))
    scale = (C_NOPE + C_ROPE) ** -0.5 * LOG2E
    qw, vw = C_HEADS * C_PAD, C_HEADS * C_V
    return pl.pallas_call(
        functools.partial(_cprep_kernel, scale=scale),
        grid=(n // tm,),
        in_specs=[pl.BlockSpec((tm, C_Q_RANK), lambda i: (i, cqo)), pl.BlockSpec((tm, C_KV_RANK), lambda i: (i, ckvo)),
                  pl.BlockSpec((tm, C_PAD), lambda i: (i, kro)), full(wq), full(wk), full(wv), full(gq), full(gkv),
                  tab_spec, tab_spec, tab_spec],
        out_specs=[pl.BlockSpec((tm, qw), lambda i: (i, 0)), pl.BlockSpec((tm, qw), lambda i: (i, 0)),
                   pl.BlockSpec((tm, vw), lambda i: (i, 0))],
        out_shape=[jax.ShapeDtypeStruct((n, qw), BF16), jax.ShapeDtypeStruct((n, qw), BF16),
                   jax.ShapeDtypeStruct((n, vw), BF16)],
        compiler_params=_cparams(("parallel",)),
        name="c_prep",
    )(h, h, h, wq, wk, wv, gq, gkv, *tabs)


WIN_QB = 2
ONES_ROWS = 16


def _reduce_rows(x, op):
    acc = x[:8]
    for r in range(8, x.shape[0], 8):
        acc = op(acc, x[r:r + 8])
    while acc.shape[0] > 1:
        half = acc.shape[0] // 2
        acc = op(acc[:half], acc[half:])
    return acc[0]


def _win_kernel(q_ref, kp_ref, kc_ref, kn_ref, vtp_ref, vtc_ref, vtn_ref, bias_ref, sink_ref, o_ref, *, nb, scale):
    step = pl.program_id(1)
    g = A_HEADS // A_KV_HEADS
    ones = jnp.ones((ONES_ROWS, 3 * BLOCK), BF16)
    for u in range(WIN_QB):
        blk = step * WIN_QB + u
        edge = jnp.where(blk == 0, 1, jnp.where(blk == nb - 1, 2, 0))
        rs = slice(u * BLOCK, (u + 1) * BLOCK)
        for hk in range(A_KV_HEADS):
            ks = slice(hk * HEAD_DIM, (hk + 1) * HEAD_DIM)
            kparts = [kp_ref[:, ks]] + [kc_ref[w * BLOCK:(w + 1) * BLOCK, ks] for w in range(WIN_QB)] + [kn_ref[:, ks]]
            vtparts = [vtp_ref[0, hk, 0]] + [vtc_ref[0, hk, w] for w in range(WIN_QB)] + [vtn_ref[0, hk, 0]]
            k = jnp.concatenate(kparts[u:u + 3], axis=0)
            vt = jnp.concatenate([jnp.concatenate(vtparts[u:u + 3], axis=1), ones], axis=0)
            q = jnp.concatenate([q_ref[rs, (hk * g + j) * HEAD_DIM:(hk * g + j + 1) * HEAD_DIM] for j in range(g)],
                                axis=0)
            st = (lax.dot_general(k, q, (((1,), (1,)), ((), ())), preferred_element_type=F32) * scale
                  + bias_ref[edge, hk])
            sink = sink_ref[hk]
            colmax = jnp.max(_reduce_rows(st.reshape(3 * BLOCK // 8, 8, g * BLOCK), jnp.maximum), axis=0, keepdims=True)
            m = jnp.maximum(colmax, sink)
            p = jnp.exp(st - m).astype(BF16)
            acc = jnp.dot(vt, p, preferred_element_type=F32)
            denom = acc[HEAD_DIM:HEAD_DIM + 1] + jnp.exp(sink - m)
            o = (acc[:HEAD_DIM] / denom).T.astype(BF16)
            for j in range(g):
                o_ref[rs, (hk * g + j) * HEAD_DIM:(hk * g + j + 1) * HEAD_DIM] = o[j * BLOCK:(j + 1) * BLOCK]


def _window_attention(h, vta, bias, sink, bsz, seq):
    off, _ = _layout()
    nb = seq // BLOCK
    assert nb % WIN_QB == 0 and nb >= 2
    ns = nb // WIN_QB
    qw, kw = A_HEADS * HEAD_DIM, A_KV_HEADS * HEAD_DIM
    qo, ko = off['qa'][0] // qw, off['ka'][0] // kw
    cur = lambda b, n: b * ns + n
    prv = lambda n: jnp.maximum(n * WIN_QB - 1, 0)
    nxt = lambda n: jnp.minimum((n + 1) * WIN_QB, nb - 1)
    kedge = lambda f: pl.BlockSpec((BLOCK, kw), lambda b, n: (b * nb + f(n), ko))
    vedge = lambda f: pl.BlockSpec((1, A_KV_HEADS, 1, HEAD_DIM, BLOCK), lambda b, n: (b, 0, f(n), 0, 0))
    full = lambda a: pl.BlockSpec(a.shape, lambda b, n: (0,) * a.ndim)
    return pl.pallas_call(
        functools.partial(_win_kernel, nb=nb, scale=HEAD_DIM ** -0.5),
        grid=(bsz, ns),
        in_specs=[pl.BlockSpec((WIN_QB * BLOCK, qw), lambda b, n: (cur(b, n), qo)),
                  kedge(prv), pl.BlockSpec((WIN_QB * BLOCK, kw), lambda b, n: (cur(b, n), ko)), kedge(nxt),
                  vedge(prv), pl.BlockSpec((1, A_KV_HEADS, WIN_QB, HEAD_DIM, BLOCK), lambda b, n: (b, 0, n, 0, 0)),
                  vedge(nxt), full(bias), full(sink)],
        out_specs=pl.BlockSpec((WIN_QB * BLOCK, qw), lambda b, n: (cur(b, n), 0)),
        out_shape=jax.ShapeDtypeStruct((bsz * seq, qw), BF16),
        compiler_params=_cparams(("parallel", "parallel")),
        name="window_attn",
    )(h, h, h, h, vta, vta, vta, bias, sink)


Q_CHUNK = 2 * LANE
KV_SUB = 512
KV_MIN_TRIPS = 3


def _flash_kernel(q_ref, k_ref, vt_ref, o_ref, m_sc, acc_sc, s0_sc, s1_sc, mx0_sc, mx1_sc, *,
                  nsub, nchunk, g, dq, dv):
    ki = pl.program_id(3)

    @pl.when(ki == 0)
    def _():
        m_sc[...] = jnp.full(m_sc.shape, -jnp.inf, F32)
        acc_sc[...] = jnp.zeros(acc_sc.shape, F32)

    def q_chunk(j):
        return q_ref[:, j * dq:(j + 1) * dq] if g > 1 else q_ref[j * Q_CHUNK:(j + 1) * Q_CHUNK, :]

    def scores(t, s_ref, mx_ref):
        k = k_ref[pl.ds(pl.multiple_of(t * KV_SUB, KV_SUB), KV_SUB), :]
        for j in range(nchunk):
            cs = slice(j * Q_CHUNK, (j + 1) * Q_CHUNK)
            st = lax.dot_general(k, q_chunk(j), (((1,), (1,)), ((), ())), preferred_element_type=F32)
            s_ref[:, cs] = st
            mx_ref[:, cs] = _reduce_rows(st.reshape(KV_SUB // 8, 8, Q_CHUNK), jnp.maximum)

    def softmax_pv(t, s_ref, mx_ref):
        vt = jnp.concatenate([vt_ref[0, 0, t], jnp.ones((ONES_ROWS, KV_SUB), BF16)], axis=0)
        for j in range(nchunk):
            cs = slice(j * Q_CHUNK, (j + 1) * Q_CHUNK)
            m_prev = m_sc[:, cs]
            m_new = jnp.maximum(m_prev, jnp.max(mx_ref[:, cs], axis=0, keepdims=True))
            alpha = jnp.exp2(m_prev - m_new)
            p = jnp.exp2(s_ref[:, cs] - m_new).astype(BF16)
            acc_sc[:, cs] = alpha * acc_sc[:, cs] + jnp.dot(vt, p, preferred_element_type=F32)
            m_sc[:, cs] = m_new

    bufs = ((s0_sc, mx0_sc), (s1_sc, mx1_sc))
    scores(0, *bufs[0])

    unroll = 8 if (nsub - 1) // 8 >= KV_MIN_TRIPS else 4

    def body(i, carry):
        t = unroll * i
        for u in range(unroll):
            scores(t + u + 1, *bufs[(u + 1) % 2])
            softmax_pv(t + u, *bufs[u % 2])
        return carry

    trips = (nsub - 1) // unroll
    lax.fori_loop(0, trips, body, 0)
    for t in range(trips * unroll, nsub):
        if t + 1 < nsub:
            scores(t + 1, *bufs[(t + 1) % 2])
        softmax_pv(t, *bufs[t % 2])

    @pl.when(ki == pl.num_programs(3) - 1)
    def _():
        for j in range(nchunk):
            cs = slice(j * Q_CHUNK, (j + 1) * Q_CHUNK)
            o = (acc_sc[:dv, cs] / acc_sc[dv:dv + 1, cs]).T.astype(o_ref.dtype)
            if g > 1:
                o_ref[:, j * dv:(j + 1) * dv] = o
            else:
                o_ref[j * Q_CHUNK:(j + 1) * Q_CHUNK, :] = o


def _kv_major_t(v, bsz, seq, hk, dv):
    sub = min(KV_SUB, seq)
    return v.reshape(bsz, seq // sub, sub, hk, dv).transpose(0, 3, 1, 4, 2)


def _flash(q, k, vt, bsz, seq, hq, hk, dq, dv, bk):
    g = hq // hk
    assert seq % KV_SUB == 0 and (g == 1 or g * Q_CHUNK <= 4 * Q_CHUNK)
    bk = min(bk, seq)
    nchunk = g if g > 1 else 4
    bq = Q_CHUNK if g > 1 else nchunk * Q_CHUNK
    assert seq % bq == 0 and seq % bk == 0 and bk % (2 * KV_SUB) == 0
    nq, nk, nsub = seq // bq, seq // bk, bk // KV_SUB
    return pl.pallas_call(
        functools.partial(_flash_kernel, nsub=nsub, nchunk=nchunk, g=g, dq=dq, dv=dv),
        grid=(bsz, hk, nq, nk),
        in_specs=[pl.BlockSpec((bq, g * dq), lambda b, h, i, j: (b * nq + i, h)),
                  pl.BlockSpec((bk, dq), lambda b, h, i, j: (b * nk + j, h)),
                  pl.BlockSpec((1, 1, nsub, dv, KV_SUB), lambda b, h, i, j: (b, h, j, 0, 0))],
        out_specs=pl.BlockSpec((bq, g * dv), lambda b, h, i, j: (b * nq + i, h)),
        out_shape=jax.ShapeDtypeStruct((bsz * seq, hq * dv), BF16),
        scratch_shapes=[pltpu.VMEM((1, nchunk * Q_CHUNK), F32),
                        pltpu.VMEM((dv + ONES_ROWS, nchunk * Q_CHUNK), F32), pltpu.VMEM((KV_SUB, nchunk * Q_CHUNK), F32),
                        pltpu.VMEM((KV_SUB, nchunk * Q_CHUNK), F32), pltpu.VMEM((8, nchunk * Q_CHUNK), F32),
                        pltpu.VMEM((8, nchunk * Q_CHUNK), F32)],
        compiler_params=_cparams(("parallel", "parallel", "parallel", "arbitrary")),
        name="dense_attn",
    )(q, k, vt)


def _merge_kernel(oa_ref, ob_ref, oc_ref, wa_ref, wb_ref, wc_ref, ga_ref, gb_ref, gc_ref, ba_ref, bb_ref, bc_ref,
                  o_ref):
    def branch(o_r, w_r, g_r, b_r):
        gate = jax.nn.sigmoid(g_r[...].astype(F32) + b_r[...])
        return gate * jnp.dot(o_r[...], w_r[...], preferred_element_type=F32)

    o_ref[...] = (branch(oa_ref, wa_ref, ga_ref, ba_ref) + branch(ob_ref, wb_ref, gb_ref, bb_ref)
                  + branch(oc_ref, wc_ref, gc_ref, bc_ref)).astype(o_ref.dtype)


def _merge(oa, ob, oc, wa, wb, wc, h, b_gate, tm, tn):
    n, kdim = oa.shape
    d = D_MODEL
    tm, tn = min(tm, n), min(tn, d)
    nj = d // tn
    act = pl.BlockSpec((tm, kdim), lambda i, j: (i, 0))
    wsp = pl.BlockSpec((kdim, tn), lambda i, j: (0, j))
    gate = lambda br: pl.BlockSpec((tm, tn), lambda i, j: (i, br * nj + j))
    bias = lambda br: pl.BlockSpec((1, tn), lambda i, j: (0, br * nj + j))
    return pl.pallas_call(
        _merge_kernel,
        grid=(n // tm, nj),
        in_specs=[act, act, act, wsp, wsp, wsp, gate(0), gate(1), gate(2), bias(0), bias(1), bias(2)],
        out_specs=pl.BlockSpec((tm, tn), lambda i, j: (i, j)),
        out_shape=jax.ShapeDtypeStruct((n, d), BF16),
        compiler_params=_cparams(("parallel", "parallel")),
        name="branch_merge",
    )(oa, ob, oc, wa, wb, wc, h, h, h, b_gate, b_gate, b_gate)


def _layer_norm(z, g, b):
    zc = z - jnp.mean(z, axis=-1, keepdims=True)
    var = jnp.mean(zc * zc, axis=-1, keepdims=True)
    return zc * lax.rsqrt(var + EPS) * g + b


def _woln_kernel(m_ref, wo_ref, x_ref, g_ref, b_ref, wr_ref, x1_ref, acc_ref, lg_ref, *, alpha):
    y = jnp.dot(m_ref[...], wo_ref[...], preferred_element_type=F32)
    x1 = _layer_norm(alpha * x_ref[...] + y, g_ref[...], b_ref[...])
    x1_ref[...] = x1
    acc_ref[...] = alpha * x1
    lg_ref[...] = jnp.dot(x1.astype(BF16), wr_ref[...], preferred_element_type=F32)


def _wo_ln(merged, wo, x, g, b, wr, tm):
    n, d = x.shape
    tm = min(tm, n)
    row = lambda w: pl.BlockSpec((tm, w), lambda i: (i, 0))
    full = lambda a: pl.BlockSpec(a.shape, lambda i: (0, 0))
    return pl.pallas_call(
        functools.partial(_woln_kernel, alpha=_dn_alpha()),
        grid=(n // tm,),
        in_specs=[row(d), full(wo), row(d), full(g), full(b), full(wr)],
        out_specs=[row(d), row(d), row(LANE)],
        out_shape=[jax.ShapeDtypeStruct((n, d), F32), jax.ShapeDtypeStruct((n, d), F32),
                   jax.ShapeDtypeStruct((n, LANE), F32)],
        compiler_params=_cparams(("parallel",)),
        name="wo_ln_router",
    )(merged, wo, x, g, b, wr)


def _route_kernel(lg_ref, aff_ref, thr_ref, ngt_ref, *, cap):
    lg = lg_ref[...]
    e = jnp.exp(lg - jnp.max(lg, axis=0, keepdims=True))
    aff = e / jnp.sum(e, axis=0, keepdims=True)
    aff_ref[...] = aff
    bits = lax.bitcast_convert_type(aff, jnp.int32)

    def step(i, t):
        cand = t | (jnp.int32(1) << (30 - i))
        cnt = jnp.sum((bits >= cand).astype(jnp.int32), axis=1, keepdims=True)
        return jnp.where(cnt >= cap, cand, t)

    thr = lax.fori_loop(0, 31, step, jnp.zeros((lg.shape[0], 1), jnp.int32))
    ngt = jnp.sum((bits > thr).astype(jnp.int32), axis=1, keepdims=True)
    thr_ref[...] = jnp.broadcast_to(thr, thr_ref.shape)
    ngt_ref[...] = jnp.broadcast_to(ngt, ngt_ref.shape)


def _route(logits_t, cap):
    e, n = logits_t.shape
    full = lambda s: pl.BlockSpec(s, lambda: (0, 0))
    return pl.pallas_call(
        functools.partial(_route_kernel, cap=cap),
        in_specs=[full((e, n))],
        out_specs=[full((e, n)), full((e, LANE)), full((e, LANE))],
        out_shape=[jax.ShapeDtypeStruct((e, n), F32), jax.ShapeDtypeStruct((e, LANE), jnp.int32),
                   jax.ShapeDtypeStruct((e, LANE), jnp.int32)],
        compiler_params=_cparams(None),
        name="route_select",
    )(logits_t)


def _select_indices(aff, thr, ngt, cap):
    e, n = aff.shape
    bits = lax.bitcast_convert_type(aff, jnp.int32)
    t, g = thr[:, :1], ngt[:, :1]
    gt, eq = bits > t, bits == t
    sel = gt | (eq & (jnp.cumsum(eq.astype(jnp.int32), axis=1) <= cap - g))
    tok = jnp.arange(n, dtype=jnp.int32)[None]
    idx, gval = lax.sort((jnp.where(sel, tok, tok + n), aff), dimension=1, num_keys=1)
    return idx[:, :cap], gval[:, :cap]


MOE_DMA_UNROLL = 64
SUBLANES = 8


def _moe_kernel(idx_ref, nxt_ref, x_hbm, acc_in, gv_ref, wg_ref, wu_ref, wd_ref, acc_out, xbuf, obuf, xb, hacc,
                sem_x, sem_o, sem_w, *, tm):
    e, i, f = pl.program_id(0), pl.program_id(1), pl.program_id(2)
    nf = pl.num_programs(2)
    tile = e * pl.num_programs(1) + i
    last_tile = pl.num_programs(0) * pl.num_programs(1) - 1
    d = xb.shape[1]

    def rows(ids_ref, make_copy):
        def issue(r0, c):
            for u in range(MOE_DMA_UNROLL):
                t = ids_ref[0, 0, r0 * MOE_DMA_UNROLL + u]
                hbm_row = (t >> 3, pl.ds(t & (SUBLANES - 1), 1))
                vmem_row = (r0 * (MOE_DMA_UNROLL // SUBLANES) + u // SUBLANES, pl.ds(u % SUBLANES, 1))
                make_copy(hbm_row, vmem_row).start()
            return c

        lax.fori_loop(0, tm // MOE_DMA_UNROLL, issue, 0)

    x_row = lambda hr, vr: pltpu.make_async_copy(x_hbm.at[hr], xbuf.at[vr], sem_x)
    acc_row = lambda hr, vr: pltpu.make_async_copy(acc_in.at[hr], obuf.at[vr], sem_o)
    out_row = lambda hr, vr: pltpu.make_async_copy(obuf.at[vr], acc_out.at[hr], sem_w)
    whole = pl.ds(0, tm // SUBLANES)
    wait_x = lambda: pltpu.make_async_copy(x_hbm.at[whole], xbuf, sem_x).wait()
    wait_acc = lambda: pltpu.make_async_copy(acc_in.at[whole], obuf, sem_o).wait()
    wait_out = lambda: pltpu.make_async_copy(obuf, acc_out.at[whole], sem_w).wait()

    @pl.when((f == 0) & (tile == 0))
    def _():
        rows(idx_ref, x_row)

    @pl.when(f == 0)
    def _():
        wait_x()
        xb[...] = xbuf[...].reshape(tm, d).astype(BF16)
        hacc[...] = jnp.zeros(hacc.shape, F32)

    @pl.when((f == 1) & (tile > 0))
    def _():
        wait_out()

    @pl.when(f == 1)
    def _():
        rows(idx_ref, acc_row)

    @pl.when((f == 2) & (tile < last_tile))
    def _():
        rows(nxt_ref, x_row)

    x = xb[...]
    gate = jnp.dot(x, wg_ref[0], preferred_element_type=F32)
    up = jnp.dot(x, wu_ref[0], preferred_element_type=F32)
    hh = (gate * jax.nn.sigmoid(gate) * up).astype(BF16)
    hacc[...] += jnp.dot(hh, wd_ref[0], preferred_element_type=F32)

    @pl.when(f == nf - 1)
    def _():
        wait_acc()
        obuf[...] = obuf[...] + (hacc[...] * gv_ref[0]).reshape(obuf.shape)
        rows(idx_ref, out_row)

    @pl.when((f == nf - 1) & (tile == last_tile))
    def _():
        wait_out()


def _moe_ffn(x1, acc0, idx, gval, wg, wu, wd, tm, tf):
    n, d = x1.shape
    e, cap = idx.shape
    ff = wg.shape[2]
    tm, tf = min(tm, cap), min(tf, ff)
    nt, nf = cap // tm, ff // tf
    assert nf >= 4 and tm % MOE_DMA_UNROLL == 0
    assert n % SUBLANES == 0 and MOE_DMA_UNROLL % SUBLANES == 0
    rows3 = lambda a: a.reshape(n // SUBLANES, SUBLANES, d)
    idx3 = idx.reshape(e * nt, 1, tm)
    gv3 = gval.reshape(e * nt, tm, 1)
    anyspec = pl.BlockSpec(memory_space=pl.ANY)
    last = e * nt - 1
    return pl.pallas_call(
        functools.partial(_moe_kernel, tm=tm),
        grid=(e, nt, nf),
        in_specs=[pl.BlockSpec((1, 1, tm), lambda ei, i, f: (ei * nt + i, 0, 0), memory_space=pltpu.SMEM),
                  pl.BlockSpec((1, 1, tm), lambda ei, i, f: (jnp.minimum(ei * nt + i + 1, last), 0, 0),
                               memory_space=pltpu.SMEM),
                  anyspec, anyspec,
                  pl.BlockSpec((1, tm, 1), lambda ei, i, f: (ei * nt + i, 0, 0)),
                  pl.BlockSpec((1, d, tf), lambda ei, i, f: (ei, 0, f)),
                  pl.BlockSpec((1, d, tf), lambda ei, i, f: (ei, 0, f)),
                  pl.BlockSpec((1, tf, d), lambda ei, i, f: (ei, f, 0))],
        out_specs=anyspec,
        out_shape=jax.ShapeDtypeStruct((n // SUBLANES, SUBLANES, d), F32),
        scratch_shapes=[pltpu.VMEM((tm // SUBLANES, SUBLANES, d), F32), pltpu.VMEM((tm // SUBLANES, SUBLANES, d), F32),
                        pltpu.VMEM((tm, d), BF16),
                        pltpu.VMEM((tm, d), F32), pltpu.SemaphoreType.DMA(()), pltpu.SemaphoreType.DMA(()),
                        pltpu.SemaphoreType.DMA(())],
        input_output_aliases={3: 0},
        compiler_params=_cparams(("arbitrary", "arbitrary", "arbitrary")),
        name="moe_ffn",
    )(idx3, idx3, rows3(x1), rows3(acc0), gv3, wg, wu, wd).reshape(n, d)


def _ln_kernel(z_ref, g_ref, b_ref, x_ref, xb_ref):
    x = _layer_norm(z_ref[...], g_ref[...], b_ref[...])
    x_ref[...] = x
    xb_ref[...] = x.astype(BF16)


def _ln_out(z, g, b, tm):
    n, d = z.shape
    tm = min(tm, n)
    row = pl.BlockSpec((tm, d), lambda i: (i, 0))
    vec = pl.BlockSpec((1, d), lambda i: (0, 0))
    return pl.pallas_call(
        _ln_kernel,
        grid=(n // tm,),
        in_specs=[row, vec, vec],
        out_specs=[row, row],
        out_shape=[jax.ShapeDtypeStruct((n, d), F32), jax.ShapeDtypeStruct((n, d), BF16)],
        compiler_params=_cparams(("parallel",)),
        name="ln_out",
    )(z, g, b)


def _t5_bucket(rel):
    nb = N_BUCKETS // 2
    max_exact = nb // 2
    n = jnp.abs(rel)
    scaled = jnp.log(jnp.maximum(n, 1).astype(F32) / max_exact) / math.log(MAX_DIST / max_exact)
    large = jnp.minimum(max_exact + (scaled * (nb - max_exact)).astype(jnp.int32), nb - 1)
    return jnp.where(rel > 0, nb, 0) + jnp.where(n < max_exact, n, large)


def _rope_tabs(ang_first, ang_second):
    def half(ang):
        if ang is None:
            return None
        return jnp.concatenate([ang, ang], axis=-1)
    a1, a2 = half(ang_first), half(ang_second)
    s = a1.shape[0]
    if a2 is None:
        cos = jnp.concatenate([jnp.cos(a1), jnp.ones((s, 64), F32)], -1)
        sin = jnp.concatenate([jnp.sin(a1), jnp.zeros((s, 64), F32)], -1)
    else:
        cos = jnp.concatenate([jnp.cos(a1), jnp.cos(a2)], -1)
        sin = jnp.concatenate([jnp.sin(a1), jnp.sin(a2)], -1)
    first = (jnp.arange(LANE) % 64) < 32
    return cos, jnp.where(first, -sin, 0.0), jnp.where(first, 0.0, sin)


def _tables(seq):
    rows = seq // GRID_W
    row = jnp.repeat(jnp.arange(rows), GRID_W).astype(F32)
    col = jnp.tile(jnp.arange(GRID_W), rows).astype(F32)
    t = jnp.arange(seq).astype(F32)
    inv_ax = ROPE_THETA ** (-jnp.arange(0, HEAD_DIM // 2, 2, dtype=F32) / (HEAD_DIM // 2))
    inv_t = ROPE_THETA ** (-jnp.arange(0, C_ROPE, 2, dtype=F32) / C_ROPE)
    ax = _rope_tabs(row[:, None] * inv_ax[None], col[:, None] * inv_ax[None])
    tt = _rope_tabs(t[:, None] * inv_t[None], None)
    return ax, tt


def _prep_weights(p):
    off, total = _layout()
    d = D_MODEL
    L = p['w_in'].shape[0]
    splits = np.cumsum([A_HEADS * HEAD_DIM, A_KV_HEADS * HEAD_DIM, A_KV_HEADS * HEAD_DIM, B_HEADS * HEAD_DIM,
                        B_KV_HEADS * HEAD_DIM, B_KV_HEADS * HEAD_DIM, C_Q_RANK, C_KV_RANK, C_ROPE])
    qa, ka, va, qb, kb, vb, cq, ckv, kr, g = jnp.split(p['w_in'].astype(BF16), [int(s) for s in splits], axis=-1)
    z = lambda w: jnp.zeros((L, d, w), BF16)
    krp = jnp.concatenate([z(LANE), kr, z(C_PAD - LANE - C_ROPE)], -1)
    cols = dict(g=g, qa=qa, qb=qb, cq=cq, ckv=ckv, ka=ka, va=va, kb=kb, vb=vb, kr=krp)
    parts, o = [], 0
    for name, (start, width) in off.items():
        assert start == o and cols[name].shape[-1] == width
        parts.append(cols[name])
        o += width
    parts.append(z(total - o))
    w_in = jnp.concatenate(parts, -1)

    wuq = p['w_uq'].astype(BF16).reshape(L, C_Q_RANK, C_HEADS, C_NOPE + C_ROPE)
    wuq = jnp.pad(wuq, ((0, 0), (0, 0), (0, 0), (0, C_PAD - C_NOPE - C_ROPE))).reshape(L, C_Q_RANK, C_HEADS * C_PAD)
    wukv = p['w_ukv'].astype(BF16).reshape(L, C_KV_RANK, C_HEADS, C_NOPE + C_V)
    wk = wukv[..., :C_NOPE].reshape(L, C_KV_RANK, C_HEADS * C_NOPE)
    wv = wukv[..., C_NOPE:].reshape(L, C_KV_RANK, C_HEADS * C_V)
    wr = jnp.pad(p['w_router'], ((0, 0), (0, 0), (0, LANE - N_EXPERTS))).astype(BF16)
    row = lambda a: a[:, None, :]
    g_heads = A_HEADS // A_KV_HEADS
    sink = jnp.broadcast_to(p['a_sink'].reshape(L, A_KV_HEADS, 1, g_heads, 1),
                            (L, A_KV_HEADS, 1, g_heads, BLOCK)).reshape(L, A_KV_HEADS, 1, g_heads * BLOCK)
    return dict(w_in=w_in, b_gate=row(p['b_gate']), gqb=row(p['b_q_norm']), gkb=row(p['b_k_norm']),
                gcq=row(p['c_q_norm']), gckv=row(p['c_kv_norm']), wuq=wuq, wk=wk, wv=wv, sink=sink,
                wa=p['w_branch_a'].astype(BF16), wb=p['w_branch_b'].astype(BF16), wc=p['w_branch_c'].astype(BF16),
                wo=p['w_o'].astype(BF16), ln1_g=row(p['ln1_g']), ln1_b=row(p['ln1_b']), wr=wr,
                weg=p['w_e_gate'].astype(BF16), weu=p['w_e_up'].astype(BF16), wed=p['w_e_down'].astype(BF16),
                ln2_g=row(p['ln2_g']), ln2_b=row(p['ln2_b']))


def _layer(x, xb, w, bias, tabs, bsz, seq):
    off, _ = _layout()
    n = bsz * seq
    ax_tabs, t_tabs = tabs
    h = _matmul(xb, w['w_in'], BF16, 4096, 512)
    va = h[:, off['va'][0]:off['va'][0] + off['va'][1]]
    vta = va.reshape(bsz, seq // BLOCK, BLOCK, A_KV_HEADS, HEAD_DIM).transpose(0, 3, 1, 4, 2)
    oa = _window_attention(h, vta, bias, w['sink'], bsz, seq)
    qb, kb = _b_prep(h, ax_tabs, w['gqb'], w['gkb'], seq, 512)
    vb = h[:, off['vb'][0]:off['vb'][0] + off['vb'][1]]
    ob = _flash(qb, kb, _kv_major_t(vb, bsz, seq, B_KV_HEADS, HEAD_DIM), bsz, seq, B_HEADS, B_KV_HEADS, HEAD_DIM,
                HEAD_DIM, 16384)
    qc, kc, vc = _c_prep(h, w['wuq'], w['wk'], w['wv'], w['gcq'], w['gckv'], t_tabs, seq, 512)
    oc = _flash(qc, kc, _kv_major_t(vc, bsz, seq, C_HEADS, C_V), bsz, seq, C_HEADS, C_HEADS, C_PAD, C_V, 16384)
    merged = _merge(oa, ob, oc, w['wa'], w['wb'], w['wc'], h, w['b_gate'], 1024, 1024)
    x1, acc0, logits = _wo_ln(merged, w['wo'], x, w['ln1_g'], w['ln1_b'], w['wr'], 512)
    cap = EC_FACTOR * n // N_EXPERTS
    aff, thr, ngt = _route(logits[:, :N_EXPERTS].T, cap)
    idx, gval = _select_indices(aff, thr, ngt, cap)
    acc = _moe_ffn(x1, acc0, idx, gval, w['weg'], w['weu'], w['wed'], 1024, 512)
    return _ln_out(acc, w['ln2_g'], w['ln2_b'], 512)


def _rel_bias(rel_table):
    offs = jnp.arange(3 * BLOCK)[None, :] - BLOCK - jnp.arange(BLOCK)[:, None]
    rb = jnp.transpose(rel_table[_t5_bucket(offs)], (2, 0, 1)).astype(F32)
    rb = jnp.where(jnp.abs(offs)[None] <= WINDOW, rb, NEG)
    col = jnp.arange(3 * BLOCK)
    variants = jnp.stack([rb, jnp.where(col >= BLOCK, rb, NEG), jnp.where(col < 2 * BLOCK, rb, NEG)])
    g = A_HEADS // A_KV_HEADS
    return variants.reshape(3, A_KV_HEADS, g * BLOCK, 3 * BLOCK).transpose(0, 1, 3, 2)


def kernel(x_prompt, x_sample, rel_table, w_in, b_gate, b_q_norm, b_k_norm, c_q_norm, c_kv_norm, w_uq, w_ukv, a_sink,
           w_branch_a, w_branch_b, w_branch_c, w_o, ln1_g, ln1_b, w_router, w_e_gate, w_e_up, w_e_down, ln2_g, ln2_b):
    params = dict(w_in=w_in, b_gate=b_gate, b_q_norm=b_q_norm, b_k_norm=b_k_norm, c_q_norm=c_q_norm,
                  c_kv_norm=c_kv_norm, w_uq=w_uq, w_ukv=w_ukv, a_sink=a_sink, w_branch_a=w_branch_a,
                  w_branch_b=w_branch_b, w_branch_c=w_branch_c, w_o=w_o, ln1_g=ln1_g, ln1_b=ln1_b,
                  w_router=w_router, w_e_gate=w_e_gate, w_e_up=w_e_up, w_e_down=w_e_down, ln2_g=ln2_g, ln2_b=ln2_b)
    w_all = _prep_weights(params)
    bias = _rel_bias(rel_table)
    groups = []
    for x in (x_prompt, x_sample):
        bsz, seq, d = x.shape
        groups.append((bsz, seq, _tables(seq)))
    xs = tuple(x.reshape(-1, x.shape[-1]) for x in (x_prompt, x_sample))
    carry = tuple((x, x.astype(BF16)) for x in xs)

    def body(carry, w):
        out = []
        for (x, xb), (bsz, seq, tabs) in zip(carry, groups):
            out.append(_layer(x, xb, w, bias, tabs, bsz, seq))
        return tuple(out), None

    carry, _ = lax.scan(body, carry, w_all)
    return tuple(c[0].reshape(xin.shape) for c, xin in zip(carry, (x_prompt, x_sample)))
```

```python
import functools
import math

import jax
import jax.numpy as jnp
import numpy as np
from jax import lax
from jax.experimental import pallas as pl
from jax.experimental.pallas import tpu as pltpu

D_MODEL = 2048
DEPTH = 4
GRID_W = 64
BLOCK = 128
HEAD_DIM = 128
A_HEADS = 8
A_KV_HEADS = 2
WINDOW = 128
B_HEADS = 8
B_KV_HEADS = 2
C_HEADS = 8
C_Q_RANK = 512
C_KV_RANK = 512
C_NOPE = 128
C_ROPE = 64
C_V = 128
ROPE_THETA = 10000.0
N_BUCKETS = 32
MAX_DIST = 128
N_EXPERTS = 16
EXPERT_FF = 2048
EC_FACTOR = 2
N_BRANCH = 3
EPS = 1e-6
NEG = -1e30
LOG2E = 1.4426950408889634

LANE = 128
C_PAD = 2 * LANE
VMEM_LIMIT = 56 * 1024 * 1024

BF16 = jnp.bfloat16
F32 = jnp.float32


def _dn_alpha():
    return (2 * DEPTH) ** 0.25


def _layout():
    d = D_MODEL
    segs = [('g', N_BRANCH * d), ('qa', A_HEADS * HEAD_DIM), ('qb', B_HEADS * HEAD_DIM), ('cq', C_Q_RANK),
            ('ckv', C_KV_RANK), ('ka', A_KV_HEADS * HEAD_DIM), ('va', A_KV_HEADS * HEAD_DIM),
            ('kb', B_KV_HEADS * HEAD_DIM), ('vb', B_KV_HEADS * HEAD_DIM), ('kr', C_PAD)]
    off, o = {}, 0
    for name, w in segs:
        off[name] = (o, w)
        o += w
    total = -(-o // 512) * 512
    return off, total


def _cparams(sem, vmem=VMEM_LIMIT):
    return pltpu.CompilerParams(dimension_semantics=sem, vmem_limit_bytes=vmem)


def _mm_kernel(x_ref, w_ref, o_ref):
    o_ref[...] = jnp.dot(x_ref[...], w_ref[...], preferred_element_type=F32).astype(o_ref.dtype)


def _matmul(x, w, out_dtype, tm, tn):
    m, k = x.shape
    n = w.shape[1]
    tm, tn = min(tm, m), min(tn, n)
    return pl.pallas_call(
        _mm_kernel,
        grid=(m // tm, n // tn),
        in_specs=[pl.BlockSpec((tm, k), lambda i, j: (i, 0)), pl.BlockSpec((k, tn), lambda i, j: (0, j))],
        out_specs=pl.BlockSpec((tm, tn), lambda i, j: (i, j)),
        out_shape=jax.ShapeDtypeStruct((m, n), out_dtype),
        compiler_params=_cparams(("parallel", "parallel")),
        name="in_proj",
    )(x, w)


def _rope(x, cos, sa, sb):
    return x * cos + pltpu.roll(x, LANE - 32, 1) * sa + pltpu.roll(x, 32, 1) * sb


def _rms(x, g):
    return x * lax.rsqrt(jnp.mean(x * x, axis=-1, keepdims=True) + EPS) * g


def _bprep_kernel(q_ref, k_ref, cos_ref, sa_ref, sb_ref, gq_ref, gk_ref, qo_ref, ko_ref, *, scale):
    cos, sa, sb = cos_ref[...], sa_ref[...], sb_ref[...]
    gq, gk = gq_ref[...], gk_ref[...]
    for h in range(B_HEADS):
        x = q_ref[:, h * LANE:(h + 1) * LANE].astype(F32)
        qo_ref[:, h * LANE:(h + 1) * LANE] = (_rope(_rms(x, gq), cos, sa, sb) * scale).astype(BF16)
    for h in range(B_KV_HEADS):
        x = k_ref[:, h * LANE:(h + 1) * LANE].astype(F32)
        ko_ref[:, h * LANE:(h + 1) * LANE] = _rope(_rms(x, gk), cos, sa, sb).astype(BF16)


def _b_prep(h, tabs, gq, gk, seq, tm):
    off, _ = _layout()
    n = h.shape[0]
    tm = min(tm, seq)
    spb = seq // tm
    qw, kw = B_HEADS * HEAD_DIM, B_KV_HEADS * HEAD_DIM
    qo, ko = off['qb'][0] // qw, off['kb'][0] // kw
    tab_spec = pl.BlockSpec((tm, LANE), lambda i: (i % spb, 0))
    vec_spec = pl.BlockSpec((1, LANE), lambda i: (0, 0))
    scale = HEAD_DIM ** -0.5 * LOG2E
    return pl.pallas_call(
        functools.partial(_bprep_kernel, scale=scale),
        grid=(n // tm,),
        in_specs=[pl.BlockSpec((tm, qw), lambda i: (i, qo)), pl.BlockSpec((tm, kw), lambda i: (i, ko)),
                  tab_spec, tab_spec, tab_spec, vec_spec, vec_spec],
        out_specs=[pl.BlockSpec((tm, qw), lambda i: (i, 0)), pl.BlockSpec((tm, kw), lambda i: (i, 0))],
        out_shape=[jax.ShapeDtypeStruct((n, qw), BF16), jax.ShapeDtypeStruct((n, kw), BF16)],
        compiler_params=_cparams(("parallel",)),
        name="b_prep",
    )(h, h, *tabs, gq, gk)


def _cprep_kernel(cq_ref, ckv_ref, kr_ref, wq_ref, wk_ref, wv_ref, gq_ref, gkv_ref, cos_ref, sa_ref, sb_ref,
                  qo_ref, ko_ref, vo_ref, *, scale):
    cos, sa, sb = cos_ref[...], sa_ref[...], sb_ref[...]
    cqn = _rms(cq_ref[...].astype(F32), gq_ref[...]).astype(BF16)
    ckvn = _rms(ckv_ref[...].astype(F32), gkv_ref[...]).astype(BF16)
    q = jnp.dot(cqn, wq_ref[...], preferred_element_type=F32)
    kn = jnp.dot(ckvn, wk_ref[...], preferred_element_type=F32)
    vo_ref[...] = jnp.dot(ckvn, wv_ref[...], preferred_element_type=F32).astype(BF16)
    kr = _rope(kr_ref[:, LANE:].astype(F32), cos, sa, sb).astype(BF16)
    for h in range(C_HEADS):
        b = h * C_PAD
        qo_ref[:, b:b + LANE] = (q[:, b:b + LANE] * scale).astype(BF16)
        qo_ref[:, b + LANE:b + C_PAD] = (_rope(q[:, b + LANE:b + C_PAD], cos, sa, sb) * scale).astype(BF16)
        ko_ref[:, b:b + LANE] = kn[:, h * LANE:(h + 1) * LANE].astype(BF16)
        ko_ref[:, b + LANE:b + C_PAD] = kr


def _c_prep(h, wq, wk, wv, gq, gkv, tabs, seq, tm):
    off, _ = _layout()
    n = h.shape[0]
    tm = min(tm, seq)
    spb = seq // tm
    cqo, ckvo, kro = off['cq'][0] // C_Q_RANK, off['ckv'][0] // C_KV_RANK, off['kr'][0] // C_PAD
    tab_spec = pl.BlockSpec((tm, LANE), lambda i: (i % spb, 0))
    full = lambda a: pl.BlockSpec(a.shape, lambda i: (0, 0))
    scale = (C_NOPE + C_ROPE) ** -0.5 * LOG2E
    qw, vw = C_HEADS * C_PAD, C_HEADS * C_V
    return pl.pallas_call(
        functools.partial(_cprep_kernel, scale=scale),
        grid=(n // tm,),
        in_specs=[pl.BlockSpec((tm, C_Q_RANK), lambda i: (i, cqo)), pl.BlockSpec((tm, C_KV_RANK), lambda i: (i, ckvo)),
                  pl.BlockSpec((tm, C_PAD), lambda i: (i, kro)), full(wq), full(wk), full(wv), full(gq), full(gkv),
                  tab_spec, tab_spec, tab_spec],
        out_specs=[pl.BlockSpec((tm, qw), lambda i: (i, 0)), pl.BlockSpec((tm, qw), lambda i: (i, 0)),
                   pl.BlockSpec((tm, vw), lambda i: (i, 0))],
        out_shape=[jax.ShapeDtypeStruct((n, qw), BF16), jax.ShapeDtypeStruct((n, qw), BF16),
                   jax.ShapeDtypeStruct((n, vw), BF16)],
        compiler_params=_cparams(("parallel",)),
        name="c_prep",
    )(h, h, h, wq, wk, wv, gq, gkv, *tabs)


WIN_QB = 2
ONES_ROWS = 16


def _reduce_rows(x, op):
    acc = x[:8]
    for r in range(8, x.shape[0], 8):
        acc = op(acc, x[r:r + 8])
    while acc.shape[0] > 1:
        half = acc.shape[0] // 2
        acc = op(acc[:half], acc[half:])
    return acc[0]


def _win_kernel(q_ref, kp_ref, kc_ref, kn_ref, vtp_ref, vtc_ref, vtn_ref, bias_ref, sink_ref, o_ref, *, nb, scale):
    step = pl.program_id(1)
    g = A_HEADS // A_KV_HEADS
    ones = jnp.ones((ONES_ROWS, 3 * BLOCK), BF16)
    for u in range(WIN_QB):
        blk = step * WIN_QB + u
        edge = jnp.where(blk == 0, 1, jnp.where(blk == nb - 1, 2, 0))
        rs = slice(u * BLOCK, (u + 1) * BLOCK)
        for hk in range(A_KV_HEADS):
            ks = slice(hk * HEAD_DIM, (hk + 1) * HEAD_DIM)
            kparts = [kp_ref[:, ks]] + [kc_ref[w * BLOCK:(w + 1) * BLOCK, ks] for w in range(WIN_QB)] + [kn_ref[:, ks]]
            vtparts = [vtp_ref[0, hk, 0]] + [vtc_ref[0, hk, w] for w in range(WIN_QB)] + [vtn_ref[0, hk, 0]]
            k = jnp.concatenate(kparts[u:u + 3], axis=0)
            vt = jnp.concatenate([jnp.concatenate(vtparts[u:u + 3], axis=1), ones], axis=0)
            q = jnp.concatenate([q_ref[rs, (hk * g + j) * HEAD_DIM:(hk * g + j + 1) * HEAD_DIM] for j in range(g)],
                                axis=0)
            st = (lax.dot_general(k, q, (((1,), (1,)), ((), ())), preferred_element_type=F32) * scale
                  + bias_ref[edge, hk])
            sink = sink_ref[hk]
            colmax = jnp.max(_reduce_rows(st.reshape(3 * BLOCK // 8, 8, g * BLOCK), jnp.maximum), axis=0, keepdims=True)
            m = jnp.maximum(colmax, sink)
            p = jnp.exp(st - m).astype(BF16)
            acc = jnp.dot(vt, p, preferred_element_type=F32)
            denom = acc[HEAD_DIM:HEAD_DIM + 1] + jnp.exp(sink - m)
            o = (acc[:HEAD_DIM] / denom).T.astype(BF16)
            for j in range(g):
                o_ref[rs, (hk * g + j) * HEAD_DIM:(hk * g + j + 1) * HEAD_DIM] = o[j * BLOCK:(j + 1) * BLOCK]


def _window_attention(h, vta, bias, sink, bsz, seq):
    off, _ = _layout()
    nb = seq // BLOCK
    assert nb % WIN_QB == 0 and nb >= 2
    ns = nb // WIN_QB
    qw, kw = A_HEADS * HEAD_DIM, A_KV_HEADS * HEAD_DIM
    qo, ko = off['qa'][0] // qw, off['ka'][0] // kw
    cur = lambda b, n: b * ns + n
    prv = lambda n: jnp.maximum(n * WIN_QB - 1, 0)
    nxt = lambda n: jnp.minimum((n + 1) * WIN_QB, nb - 1)
    kedge = lambda f: pl.BlockSpec((BLOCK, kw), lambda b, n: (b * nb + f(n), ko))
    vedge = lambda f: pl.BlockSpec((1, A_KV_HEADS, 1, HEAD_DIM, BLOCK), lambda b, n: (b, 0, f(n), 0, 0))
    full = lambda a: pl.BlockSpec(a.shape, lambda b, n: (0,) * a.ndim)
    return pl.pallas_call(
        functools.partial(_win_kernel, nb=nb, scale=HEAD_DIM ** -0.5),
        grid=(bsz, ns),
        in_specs=[pl.BlockSpec((WIN_QB * BLOCK, qw), lambda b, n: (cur(b, n), qo)),
                  kedge(prv), pl.BlockSpec((WIN_QB * BLOCK, kw), lambda b, n: (cur(b, n), ko)), kedge(nxt),
                  vedge(prv), pl.BlockSpec((1, A_KV_HEADS, WIN_QB, HEAD_DIM, BLOCK), lambda b, n: (b, 0, n, 0, 0)),
                  vedge(nxt), full(bias), full(sink)],
        out_specs=pl.BlockSpec((WIN_QB * BLOCK, qw), lambda b, n: (cur(b, n), 0)),
        out_shape=jax.ShapeDtypeStruct((bsz * seq, qw), BF16),
        compiler_params=_cparams(("parallel", "parallel")),
        name="window_attn",
    )(h, h, h, h, vta, vta, vta, bias, sink)


Q_CHUNK = 2 * LANE
KV_SUB = 512
KV_MIN_TRIPS = 3


def _flash_kernel(q_ref, k_ref, vt_ref, o_ref, m_sc, acc_sc, s0_sc, s1_sc, mx0_sc, mx1_sc, *,
                  nsub, nchunk, g, dq, dv):
    ki = pl.program_id(3)

    @pl.when(ki == 0)
    def _():
        m_sc[...] = jnp.full(m_sc.shape, -jnp.inf, F32)
        acc_sc[...] = jnp.zeros(acc_sc.shape, F32)

    def q_chunk(j):
        return q_ref[:, j * dq:(j + 1) * dq] if g > 1 else q_ref[j * Q_CHUNK:(j + 1) * Q_CHUNK, :]

    def scores(t, s_ref, mx_ref):
        k = k_ref[pl.ds(pl.multiple_of(t * KV_SUB, KV_SUB), KV_SUB), :]
        for j in range(nchunk):
            cs = slice(j * Q_CHUNK, (j + 1) * Q_CHUNK)
            st = lax.dot_general(k, q_chunk(j), (((1,), (1,)), ((), ())), preferred_element_type=F32)
            s_ref[:, cs] = st
            mx_ref[:, cs] = _reduce_rows(st.reshape(KV_SUB // 8, 8, Q_CHUNK), jnp.maximum)

    def softmax_pv(t, s_ref, mx_ref):
        vt = jnp.concatenate([vt_ref[0, 0, t], jnp.ones((ONES_ROWS, KV_SUB), BF16)], axis=0)
        for j in range(nchunk):
            cs = slice(j * Q_CHUNK, (j + 1) * Q_CHUNK)
            m_prev = m_sc[:, cs]
            m_new = jnp.maximum(m_prev, jnp.max(mx_ref[:, cs], axis=0, keepdims=True))
            alpha = jnp.exp2(m_prev - m_new)
            p = jnp.exp2(s_ref[:, cs] - m_new).astype(BF16)
            acc_sc[:, cs] = alpha * acc_sc[:, cs] + jnp.dot(vt, p, preferred_element_type=F32)
            m_sc[:, cs] = m_new

    bufs = ((s0_sc, mx0_sc), (s1_sc, mx1_sc))
    scores(0, *bufs[0])

    unroll = 8 if (nsub - 1) // 8 >= KV_MIN_TRIPS else 4

    def body(i, carry):
        t = unroll * i
        for u in range(unroll):
            scores(t + u + 1, *bufs[(u + 1) % 2])
            softmax_pv(t + u, *bufs[u % 2])
        return carry

    trips = (nsub - 1) // unroll
    lax.fori_loop(0, trips, body, 0)
    for t in range(trips * unroll, nsub):
        if t + 1 < nsub:
            scores(t + 1, *bufs[(t + 1) % 2])
        softmax_pv(t, *bufs[t % 2])

    @pl.when(ki == pl.num_programs(3) - 1)
    def _():
        for j in range(nchunk):
            cs = slice(j * Q_CHUNK, (j + 1) * Q_CHUNK)
            o = (acc_sc[:dv, cs] / acc_sc[dv:dv + 1, cs]).T.astype(o_ref.dtype)
            if g > 1:
                o_ref[:, j * dv:(j + 1) * dv] = o
            else:
                o_ref[j * Q_CHUNK:(j + 1) * Q_CHUNK, :] = o


def _kv_major_t(v, bsz, seq, hk, dv):
    sub = min(KV_SUB, seq)
    return v.reshape(bsz, seq // sub, sub, hk, dv).transpose(0, 3, 1, 4, 2)


def _flash(q, k, vt, bsz, seq, hq, hk, dq, dv, bk):
    g = hq // hk
    assert seq % KV_SUB == 0 and (g == 1 or g * Q_CHUNK <= 4 * Q_CHUNK)
    bk = min(bk, seq)
    nchunk = g if g > 1 else 4
    bq = Q_CHUNK if g > 1 else nchunk * Q_CHUNK
    assert seq % bq == 0 and seq % bk == 0 and bk % (2 * KV_SUB) == 0
    nq, nk, nsub = seq // bq, seq // bk, bk // KV_SUB
    return pl.pallas_call(
        functools.partial(_flash_kernel, nsub=nsub, nchunk=nchunk, g=g, dq=dq, dv=dv),
        grid=(bsz, hk, nq, nk),
        in_specs=[pl.BlockSpec((bq, g * dq), lambda b, h, i, j: (b * nq + i, h)),
                  pl.BlockSpec((bk, dq), lambda b, h, i, j: (b * nk + j, h)),
                  pl.BlockSpec((1, 1, nsub, dv, KV_SUB), lambda b, h, i, j: (b, h, j, 0, 0))],
        out_specs=pl.BlockSpec((bq, g * dv), lambda b, h, i, j: (b * nq + i, h)),
        out_shape=jax.ShapeDtypeStruct((bsz * seq, hq * dv), BF16),
        scratch_shapes=[pltpu.VMEM((1, nchunk * Q_CHUNK), F32),
                        pltpu.VMEM((dv + ONES_ROWS, nchunk * Q_CHUNK), F32), pltpu.VMEM((KV_SUB, nchunk * Q_CHUNK), F32),
                        pltpu.VMEM((KV_SUB, nchunk * Q_CHUNK), F32), pltpu.VMEM((8, nchunk * Q_CHUNK), F32),
                        pltpu.VMEM((8, nchunk * Q_CHUNK), F32)],
        compiler_params=_cparams(("parallel", "parallel", "parallel", "arbitrary")),
        name="dense_attn",
    )(q, k, vt)


def _merge_kernel(oa_ref, ob_ref, oc_ref, wa_ref, wb_ref, wc_ref, ga_ref, gb_ref, gc_ref, ba_ref, bb_ref, bc_ref,
                  o_ref):
    def branch(o_r, w_r, g_r, b_r):
        gate = jax.nn.sigmoid(g_r[...].astype(F32) + b_r[...])
        return gate * jnp.dot(o_r[...], w_r[...], preferred_element_type=F32)

    o_ref[...] = (branch(oa_ref, wa_ref, ga_ref, ba_ref) + branch(ob_ref, wb_ref, gb_ref, bb_ref)
                  + branch(oc_ref, wc_ref, gc_ref, bc_ref)).astype(o_ref.dtype)


def _merge(oa, ob, oc, wa, wb, wc, h, b_gate, tm, tn):
    n, kdim = oa.shape
    d = D_MODEL
    tm, tn = min(tm, n), min(tn, d)
    nj = d // tn
    act = pl.BlockSpec((tm, kdim), lambda i, j: (i, 0))
    wsp = pl.BlockSpec((kdim, tn), lambda i, j: (0, j))
    gate = lambda br: pl.BlockSpec((tm, tn), lambda i, j: (i, br * nj + j))
    bias = lambda br: pl.BlockSpec((1, tn), lambda i, j: (0, br * nj + j))
    return pl.pallas_call(
        _merge_kernel,
        grid=(n // tm, nj),
        in_specs=[act, act, act, wsp, wsp, wsp, gate(0), gate(1), gate(2), bias(0), bias(1), bias(2)],
        out_specs=pl.BlockSpec((tm, tn), lambda i, j: (i, j)),
        out_shape=jax.ShapeDtypeStruct((n, d), BF16),
        compiler_params=_cparams(("parallel", "parallel")),
        name="branch_merge",
    )(oa, ob, oc, wa, wb, wc, h, h, h, b_gate, b_gate, b_gate)


def _layer_norm(z, g, b):
    zc = z - jnp.mean(z, axis=-1, keepdims=True)
    var = jnp.mean(zc * zc, axis=-1, keepdims=True)
    return zc * lax.rsqrt(var + EPS) * g + b


def _woln_kernel(m_ref, wo_ref, x_ref, g_ref, b_ref, wr_ref, x1_ref, acc_ref, lg_ref, *, alpha):
    y = jnp.dot(m_ref[...], wo_ref[...], preferred_element_type=F32)
    x1 = _layer_norm(alpha * x_ref[...] + y, g_ref[...], b_ref[...])
    x1_ref[...] = x1
    acc_ref[...] = alpha * x1
    lg_ref[...] = jnp.dot(x1.astype(BF16), wr_ref[...], preferred_element_type=F32)


def _wo_ln(merged, wo, x, g, b, wr, tm):
    n, d = x.shape
    tm = min(tm, n)
    row = lambda w: pl.BlockSpec((tm, w), lambda i: (i, 0))
    full = lambda a: pl.BlockSpec(a.shape, lambda i: (0, 0))
    return pl.pallas_call(
        functools.partial(_woln_kernel, alpha=_dn_alpha()),
        grid=(n // tm,),
        in_specs=[row(d), full(wo), row(d), full(g), full(b), full(wr)],
        out_specs=[row(d), row(d), row(LANE)],
        out_shape=[jax.ShapeDtypeStruct((n, d), F32), jax.ShapeDtypeStruct((n, d), F32),
                   jax.ShapeDtypeStruct((n, LANE), F32)],
        compiler_params=_cparams(("parallel",)),
        name="wo_ln_router",
    )(merged, wo, x, g, b, wr)


def _route_kernel(lg_ref, aff_ref, thr_ref, ngt_ref, *, cap):
    lg = lg_ref[...]
    e = jnp.exp(lg - jnp.max(lg, axis=0, keepdims=True))
    aff = e / jnp.sum(e, axis=0, keepdims=True)
    aff_ref[...] = aff
    bits = lax.bitcast_convert_type(aff, jnp.int32)

    def step(i, t):
        cand = t | (jnp.int32(1) << (30 - i))
        cnt = jnp.sum((bits >= cand).astype(jnp.int32), axis=1, keepdims=True)
        return jnp.where(cnt >= cap, cand, t)

    thr = lax.fori_loop(0, 31, step, jnp.zeros((lg.shape[0], 1), jnp.int32))
    ngt = jnp.sum((bits > thr).astype(jnp.int32), axis=1, keepdims=True)
    thr_ref[...] = jnp.broadcast_to(thr, thr_ref.shape)
    ngt_ref[...] = jnp.broadcast_to(ngt, ngt_ref.shape)


def _route(logits_t, cap):
    e, n = logits_t.shape
    full = lambda s: pl.BlockSpec(s, lambda: (0, 0))
    return pl.pallas_call(
        functools.partial(_route_kernel, cap=cap),
        in_specs=[full((e, n))],
        out_specs=[full((e, n)), full((e, LANE)), full((e, LANE))],
        out_shape=[jax.ShapeDtypeStruct((e, n), F32), jax.ShapeDtypeStruct((e, LANE), jnp.int32),
                   jax.ShapeDtypeStruct((e, LANE), jnp.int32)],
        compiler_params=_cparams(None),
        name="route_select",
    )(logits_t)


def _select_indices(aff, thr, ngt, cap):
    e, n = aff.shape
    bits = lax.bitcast_convert_type(aff, jnp.int32)
    t, g = thr[:, :1], ngt[:, :1]
    gt, eq = bits > t, bits == t
    sel = gt | (eq & (jnp.cumsum(eq.astype(jnp.int32), axis=1) <= cap - g))
    tok = jnp.arange(n, dtype=jnp.int32)[None]
    idx, gval = lax.sort((jnp.where(sel, tok, tok + n), aff), dimension=1, num_keys=1)
    return idx[:, :cap], gval[:, :cap]


MOE_DMA_UNROLL = 64
SUBLANES = 8


def _moe_kernel(idx_ref, nxt_ref, x_hbm, acc_in, gv_ref, wg_ref, wu_ref, wd_ref, acc_out, xbuf, obuf, xb, hacc,
                sem_x, sem_o, sem_w, *, tm):
    e, i, f = pl.program_id(0), pl.program_id(1), pl.program_id(2)
    nf = pl.num_programs(2)
    tile = e * pl.num_programs(1) + i
    last_tile = pl.num_programs(0) * pl.num_programs(1) - 1
    d = xb.shape[1]

    def rows(ids_ref, make_copy):
        def issue(r0, c):
            for u in range(MOE_DMA_UNROLL):
                t = ids_ref[0, 0, r0 * MOE_DMA_UNROLL + u]
                hbm_row = (t >> 3, pl.ds(t & (SUBLANES - 1), 1))
                vmem_row = (r0 * (MOE_DMA_UNROLL // SUBLANES) + u // SUBLANES, pl.ds(u % SUBLANES, 1))
                make_copy(hbm_row, vmem_row).start(priority=u % 2)
            return c

        lax.fori_loop(0, tm // MOE_DMA_UNROLL, issue, 0)

    x_row = lambda hr, vr: pltpu.make_async_copy(x_hbm.at[hr], xbuf.at[vr], sem_x)
    acc_row = lambda hr, vr: pltpu.make_async_copy(acc_in.at[hr], obuf.at[vr], sem_o)
    out_row = lambda hr, vr: pltpu.make_async_copy(obuf.at[vr], acc_out.at[hr], sem_w)
    whole = pl.ds(0, tm // SUBLANES)
    wait_x = lambda: pltpu.make_async_copy(x_hbm.at[whole], xbuf, sem_x).wait()
    wait_acc = lambda: pltpu.make_async_copy(acc_in.at[whole], obuf, sem_o).wait()
    wait_out = lambda: pltpu.make_async_copy(obuf, acc_out.at[whole], sem_w).wait()

    @pl.when((f == 0) & (tile == 0))
    def _():
        rows(idx_ref, x_row)

    @pl.when(f == 0)
    def _():
        wait_x()
        xb[...] = xbuf[...].reshape(tm, d).astype(BF16)
        hacc[...] = jnp.zeros(hacc.shape, F32)

    @pl.when((f == 1) & (tile > 0))
    def _():
        wait_out()

    @pl.when(f == 1)
    def _():
        rows(idx_ref, acc_row)

    @pl.when((f == 2) & (tile < last_tile))
    def _():
        rows(nxt_ref, x_row)

    x = xb[...]
    gate = jnp.dot(x, wg_ref[0], preferred_element_type=F32)
    up = jnp.dot(x, wu_ref[0], preferred_element_type=F32)
    hh = (gate * jax.nn.sigmoid(gate) * up).astype(BF16)
    hacc[...] += jnp.dot(hh, wd_ref[0], preferred_element_type=F32)

    @pl.when(f == nf - 1)
    def _():
        wait_acc()
        obuf[...] = obuf[...] + (hacc[...] * gv_ref[0]).reshape(obuf.shape)
        rows(idx_ref, out_row)

    @pl.when((f == nf - 1) & (tile == last_tile))
    def _():
        wait_out()


def _moe_ffn(x1, acc0, idx, gval, wg, wu, wd, tm, tf):
    n, d = x1.shape
    e, cap = idx.shape
    ff = wg.shape[2]
    tm, tf = min(tm, cap), min(tf, ff)
    nt, nf = cap // tm, ff // tf
    assert nf >= 4 and tm % MOE_DMA_UNROLL == 0
    assert n % SUBLANES == 0 and MOE_DMA_UNROLL % SUBLANES == 0
    rows3 = lambda a: a.reshape(n // SUBLANES, SUBLANES, d)
    idx3 = idx.reshape(e * nt, 1, tm)
    gv3 = gval.reshape(e * nt, tm, 1)
    anyspec = pl.BlockSpec(memory_space=pl.ANY)
    last = e * nt - 1
    return pl.pallas_call(
        functools.partial(_moe_kernel, tm=tm),
        grid=(e, nt, nf),
        in_specs=[pl.BlockSpec((1, 1, tm), lambda ei, i, f: (ei * nt + i, 0, 0), memory_space=pltpu.SMEM),
                  pl.BlockSpec((1, 1, tm), lambda ei, i, f: (jnp.minimum(ei * nt + i + 1, last), 0, 0),
                               memory_space=pltpu.SMEM),
                  anyspec, anyspec,
                  pl.BlockSpec((1, tm, 1), lambda ei, i, f: (ei * nt + i, 0, 0)),
                  pl.BlockSpec((1, d, tf), lambda ei, i, f: (ei, 0, f)),
                  pl.BlockSpec((1, d, tf), lambda ei, i, f: (ei, 0, f)),
                  pl.BlockSpec((1, tf, d), lambda ei, i, f: (ei, f, 0))],
        out_specs=anyspec,
        out_shape=jax.ShapeDtypeStruct((n // SUBLANES, SUBLANES, d), F32),
        scratch_shapes=[pltpu.VMEM((tm // SUBLANES, SUBLANES, d), F32), pltpu.VMEM((tm // SUBLANES, SUBLANES, d), F32),
                        pltpu.VMEM((tm, d), BF16),
                        pltpu.VMEM((tm, d), F32), pltpu.SemaphoreType.DMA(()), pltpu.SemaphoreType.DMA(()),
                        pltpu.SemaphoreType.DMA(())],
        input_output_aliases={3: 0},
        compiler_params=_cparams(("arbitrary", "arbitrary", "arbitrary")),
        name="moe_ffn",
    )(idx3, idx3, rows3(x1), rows3(acc0), gv3, wg, wu, wd).reshape(n, d)


def _ln_kernel(z_ref, g_ref, b_ref, x_ref, xb_ref):
    x = _layer_norm(z_ref[...], g_ref[...], b_ref[...])
    x_ref[...] = x
    xb_ref[...] = x.astype(BF16)


def _ln_out(z, g, b, tm):
    n, d = z.shape
    tm = min(tm, n)
    row = pl.BlockSpec((tm, d), lambda i: (i, 0))
    vec = pl.BlockSpec((1, d), lambda i: (0, 0))
    return pl.pallas_call(
        _ln_kernel,
        grid=(n // tm,),
        in_specs=[row, vec, vec],
        out_specs=[row, row],
        out_shape=[jax.ShapeDtypeStruct((n, d), F32), jax.ShapeDtypeStruct((n, d), BF16)],
        compiler_params=_cparams(("parallel",)),
        name="ln_out",
    )(z, g, b)


def _t5_bucket(rel):
    nb = N_BUCKETS // 2
    max_exact = nb // 2
    n = jnp.abs(rel)
    scaled = jnp.log(jnp.maximum(n, 1).astype(F32) / max_exact) / math.log(MAX_DIST / max_exact)
    large = jnp.minimum(max_exact + (scaled * (nb - max_exact)).astype(jnp.int32), nb - 1)
    return jnp.where(rel > 0, nb, 0) + jnp.where(n < max_exact, n, large)


def _rope_tabs(ang_first, ang_second):
    def half(ang):
        if ang is None:
            return None
        return jnp.concatenate([ang, ang], axis=-1)
    a1, a2 = half(ang_first), half(ang_second)
    s = a1.shape[0]
    if a2 is None:
        cos = jnp.concatenate([jnp.cos(a1), jnp.ones((s, 64), F32)], -1)
        sin = jnp.concatenate([jnp.sin(a1), jnp.zeros((s, 64), F32)], -1)
    else:
        cos = jnp.concatenate([jnp.cos(a1), jnp.cos(a2)], -1)
        sin = jnp.concatenate([jnp.sin(a1), jnp.sin(a2)], -1)
    first = (jnp.arange(LANE) % 64) < 32
    return cos, jnp.where(first, -sin, 0.0), jnp.where(first, 0.0, sin)


def _tables(seq):
    rows = seq // GRID_W
    row = jnp.repeat(jnp.arange(rows), GRID_W).astype(F32)
    col = jnp.tile(jnp.arange(GRID_W), rows).astype(F32)
    t = jnp.arange(seq).astype(F32)
    inv_ax = ROPE_THETA ** (-jnp.arange(0, HEAD_DIM // 2, 2, dtype=F32) / (HEAD_DIM // 2))
    inv_t = ROPE_THETA ** (-jnp.arange(0, C_ROPE, 2, dtype=F32) / C_ROPE)
    ax = _rope_tabs(row[:, None] * inv_ax[None], col[:, None] * inv_ax[None])
    tt = _rope_tabs(t[:, None] * inv_t[None], None)
    return ax, tt


def _prep_weights(p):
    off, total = _layout()
    d = D_MODEL
    L = p['w_in'].shape[0]
    splits = np.cumsum([A_HEADS * HEAD_DIM, A_KV_HEADS * HEAD_DIM, A_KV_HEADS * HEAD_DIM, B_HEADS * HEAD_DIM,
                        B_KV_HEADS * HEAD_DIM, B_KV_HEADS * HEAD_DIM, C_Q_RANK, C_KV_RANK, C_ROPE])
    qa, ka, va, qb, kb, vb, cq, ckv, kr, g = jnp.split(p['w_in'].astype(BF16), [int(s) for s in splits], axis=-1)
    z = lambda w: jnp.zeros((L, d, w), BF16)
    krp = jnp.concatenate([z(LANE), kr, z(C_PAD - LANE - C_ROPE)], -1)
    cols = dict(g=g, qa=qa, qb=qb, cq=cq, ckv=ckv, ka=ka, va=va, kb=kb, vb=vb, kr=krp)
    parts, o = [], 0
    for name, (start, width) in off.items():
        assert start == o and cols[name].shape[-1] == width
        parts.append(cols[name])
        o += width
    parts.append(z(total - o))
    w_in = jnp.concatenate(parts, -1)

    wuq = p['w_uq'].astype(BF16).reshape(L, C_Q_RANK, C_HEADS, C_NOPE + C_ROPE)
    wuq = jnp.pad(wuq, ((0, 0), (0, 0), (0, 0), (0, C_PAD - C_NOPE - C_ROPE))).reshape(L, C_Q_RANK, C_HEADS * C_PAD)
    wukv = p['w_ukv'].astype(BF16).reshape(L, C_KV_RANK, C_HEADS, C_NOPE + C_V)
    wk = wukv[..., :C_NOPE].reshape(L, C_KV_RANK, C_HEADS * C_NOPE)
    wv = wukv[..., C_NOPE:].reshape(L, C_KV_RANK, C_HEADS * C_V)
    wr = jnp.pad(p['w_router'], ((0, 0), (0, 0), (0, LANE - N_EXPERTS))).astype(BF16)
    row = lambda a: a[:, None, :]
    g_heads = A_HEADS // A_KV_HEADS
    sink = jnp.broadcast_to(p['a_sink'].reshape(L, A_KV_HEADS, 1, g_heads, 1),
                            (L, A_KV_HEADS, 1, g_heads, BLOCK)).reshape(L, A_KV_HEADS, 1, g_heads * BLOCK)
    return dict(w_in=w_in, b_gate=row(p['b_gate']), gqb=row(p['b_q_norm']), gkb=row(p['b_k_norm']),
                gcq=row(p['c_q_norm']), gckv=row(p['c_kv_norm']), wuq=wuq, wk=wk, wv=wv, sink=sink,
                wa=p['w_branch_a'].astype(BF16), wb=p['w_branch_b'].astype(BF16), wc=p['w_branch_c'].astype(BF16),
                wo=p['w_o'].astype(BF16), ln1_g=row(p['ln1_g']), ln1_b=row(p['ln1_b']), wr=wr,
                weg=p['w_e_gate'].astype(BF16), weu=p['w_e_up'].astype(BF16), wed=p['w_e_down'].astype(BF16),
                ln2_g=row(p['ln2_g']), ln2_b=row(p['ln2_b']))


def _layer(x, xb, w, bias, tabs, bsz, seq):
    off, _ = _layout()
    n = bsz * seq
    ax_tabs, t_tabs = tabs
    h = _matmul(xb, w['w_in'], BF16, 4096, 512)
    va = h[:, off['va'][0]:off['va'][0] + off['va'][1]]
    vta = va.reshape(bsz, seq // BLOCK, BLOCK, A_KV_HEADS, HEAD_DIM).transpose(0, 3, 1, 4, 2)
    oa = _window_attention(h, vta, bias, w['sink'], bsz, seq)
    qb, kb = _b_prep(h, ax_tabs, w['gqb'], w['gkb'], seq, 512)
    vb = h[:, off['vb'][0]:off['vb'][0] + off['vb'][1]]
    ob = _flash(qb, kb, _kv_major_t(vb, bsz, seq, B_KV_HEADS, HEAD_DIM), bsz, seq, B_HEADS, B_KV_HEADS, HEAD_DIM,
                HEAD_DIM, 16384)
    qc, kc, vc = _c_prep(h, w['wuq'], w['wk'], w['wv'], w['gcq'], w['gckv'], t_tabs, seq, 512)
    oc = _flash(qc, kc, _kv_major_t(vc, bsz, seq, C_HEADS, C_V), bsz, seq, C_HEADS, C_HEADS, C_PAD, C_V, 16384)
    merged = _merge(oa, ob, oc, w['wa'], w['wb'], w['wc'], h, w['b_gate'], 1024, 1024)
    x1, acc0, logits = _wo_ln(merged, w['wo'], x, w['ln1_g'], w['ln1_b'], w['wr'], 512)
    cap = EC_FACTOR * n // N_EXPERTS
    aff, thr, ngt = _route(logits[:, :N_EXPERTS].T, cap)
    idx, gval = _select_indices(aff, thr, ngt, cap)
    acc = _moe_ffn(x1, acc0, idx, gval, w['weg'], w['weu'], w['wed'], 1024, 512)
    return _ln_out(acc, w['ln2_g'], w['ln2_b'], 512)


def _rel_bias(rel_table):
    offs = jnp.arange(3 * BLOCK)[None, :] - BLOCK - jnp.arange(BLOCK)[:, None]
    rb = jnp.transpose(rel_table[_t5_bucket(offs)], (2, 0, 1)).astype(F32)
    rb = jnp.where(jnp.abs(offs)[None] <= WINDOW, rb, NEG)
    col = jnp.arange(3 * BLOCK)
    variants = jnp.stack([rb, jnp.where(col >= BLOCK, rb, NEG), jnp.where(col < 2 * BLOCK, rb, NEG)])
    g = A_HEADS // A_KV_HEADS
    return variants.reshape(3, A_KV_HEADS, g * BLOCK, 3 * BLOCK).transpose(0, 1, 3, 2)


def kernel(x_prompt, x_sample, rel_table, w_in, b_gate, b_q_norm, b_k_norm, c_q_norm, c_kv_norm, w_uq, w_ukv, a_sink,
           w_branch_a, w_branch_b, w_branch_c, w_o, ln1_g, ln1_b, w_router, w_e_gate, w_e_up, w_e_down, ln2_g, ln2_b):
    params = dict(w_in=w_in, b_gate=b_gate, b_q_norm=b_q_norm, b_k_norm=b_k_norm, c_q_norm=c_q_norm,
                  c_kv_norm=c_kv_norm, w_uq=w_uq, w_ukv=w_ukv, a_sink=a_sink, w_branch_a=w_branch_a,
                  w_branch_b=w_branch_b, w_branch_c=w_branch_c, w_o=w_o, ln1_g=ln1_g, ln1_b=ln1_b,
                  w_router=w_router, w_e_gate=w_e_gate, w_e_up=w_e_up, w_e_down=w_e_down, ln2_g=ln2_g, ln2_b=ln2_b)
    w_all = _prep_weights(params)
    bias = _rel_bias(rel_table)
    groups = []
    for x in (x_prompt, x_sample):
        bsz, seq, d = x.shape
        groups.append((bsz, seq, _tables(seq)))
    xs = tuple(x.reshape(-1, x.shape[-1]) for x in (x_prompt, x_sample))
    carry = tuple((x, x.astype(BF16)) for x in xs)

    def body(carry, w):
        out = []
        for (x, xb), (bsz, seq, tabs) in zip(carry, groups):
            out.append(_layer(x, xb, w, bias, tabs, bsz, seq))
        return tuple(out), None

    carry, _ = lax.scan(body, carry, w_all)
    return tuple(c[0].reshape(xin.shape) for c, xin in zip(carry, (x_prompt, x_sample)))
```
